```python
import jax, jax.numpy as jnp
from jax import lax
import numpy as np

D_MODEL = 1024
BATCH = 2
SEQ = 8192
DEPTH = 1

D_MIX = D_MODEL
D_HGRN = D_MIX // 2
HGRN_HEAD_DIM = 128
HGRN_HEADS = D_HGRN // HGRN_HEAD_DIM
HGRN_CHUNK = 64
D_RG = D_MIX - D_HGRN
RG_BLOCKS = 8
RG_BLOCK_DIM = D_RG // RG_BLOCKS
CONV_WIDTH = 4
RG_C = 8.0
D_IN = 4 * D_HGRN + 2 * D_RG
N_GROUPS = 4
EXPERTS_PER_GROUP = 8
N_EXPERTS = N_GROUPS * EXPERTS_PER_GROUP
TOP_K = 2
D_EXPERT = D_MODEL // 2
MOE_BLOCK = 256
EPS = 1e-6

kernel_name = "hybrid_hgrn2_rglru_hmoe_block"


def rmsnorm(x, g):
    xf = x.astype(jnp.float32)
    y = xf * lax.rsqrt(jnp.mean(xf * xf, axis=-1, keepdims=True) + EPS)
    return (y * g.astype(jnp.float32)).astype(x.dtype)


def hgrn2_chunkwise(q, k, v, log_f):
    B, S, H, K = q.shape
    V = v.shape[-1]
    N = S // HGRN_CHUNK

    def to_chunks(t):
        return t.reshape(B, N, HGRN_CHUNK, H, t.shape[-1]).transpose(1, 0, 3, 2, 4)

    mask = jnp.tril(jnp.ones((HGRN_CHUNK, HGRN_CHUNK), dtype=bool))[:, :, None]

    def step(state, inp):
        qc, kc, vc, gc = inp
        b = jnp.cumsum(gc, axis=-2)
        diff = b[:, :, :, None, :] - b[:, :, None, :, :]
        decay = jnp.exp(jnp.where(mask, diff, -jnp.inf))
        scores = jnp.einsum('bhtk,bhtsk,bhsk->bhts', qc, decay, kc)
        o = jnp.einsum('bhts,bhsv->bhtv', scores, vc) + jnp.einsum('bhtk,bhkv->bhtv', qc * jnp.exp(b), state)
        b_last = b[:, :, -1:, :]
        new_state = jnp.exp(b_last[:, :, 0, :, None]) * state + jnp.einsum('bhsk,bhsv->bhkv', kc * jnp.exp(b_last - b), vc)
        return new_state, o

    s0 = jnp.zeros((B, H, K, V), jnp.float32)
    _, o = lax.scan(step, s0, (to_chunks(q), to_chunks(k), to_chunks(v), to_chunks(log_f)))
    return o.transpose(1, 0, 3, 2, 4).reshape(B, S, H, V)


def causal_dwconv(x, w, b):
    y = lax.conv_general_dilated(x, w[:, None, :].astype(x.dtype), window_strides=(1,),
                                 padding=[(CONV_WIDTH - 1, 0)],
                                 dimension_numbers=('NWC', 'WIO', 'NWC'),
                                 feature_group_count=x.shape[-1])
    return y + b.astype(x.dtype)


def rg_lru(xc, w_r, b_r, w_i, b_i, lam):
    B, S, R = xc.shape
    xh = xc.reshape(B, S, RG_BLOCKS, RG_BLOCK_DIM)
    r = jax.nn.sigmoid(jnp.einsum('bshi,hij->bshj', xh, w_r.astype(jnp.float32)).reshape(B, S, R) + b_r.astype(jnp.float32))
    i = jax.nn.sigmoid(jnp.einsum('bshi,hij->bshj', xh, w_i.astype(jnp.float32)).reshape(B, S, R) + b_i.astype(jnp.float32))
    log_a = RG_C * r * jax.nn.log_sigmoid(lam.astype(jnp.float32))
    a = jnp.exp(log_a)
    u = jnp.sqrt(-jnp.expm1(2.0 * log_a)) * (i * xc)

    def combine(lhs, rhs):
        a1, b1 = lhs
        a2, b2 = rhs
        return a1 * a2, a2 * b1 + b2

    _, h = lax.associative_scan(combine, (a, u), axis=1)
    return h


def hier_moe(h, w_rg, b_rg, w_re, b_re, w1, w3, w2):
    T, D = h.shape
    logits_g = (h @ w_rg + b_rg).astype(jnp.float32)
    p_g = jax.nn.softmax(logits_g, axis=-1)
    g_star = jnp.argmax(p_g, axis=-1)
    gate_g = jnp.max(p_g, axis=-1)
    logits_e = (h @ w_re + b_re).astype(jnp.float32).reshape(T, N_GROUPS, EXPERTS_PER_GROUP)
    sel = jnp.take_along_axis(logits_e, g_star[:, None, None], axis=1)[:, 0]
    p_e = jax.nn.softmax(sel, axis=-1)
    top_v, top_i = lax.top_k(p_e, TOP_K)
    w_tok = top_v / jnp.sum(top_v, axis=-1, keepdims=True) * gate_g[:, None]

    eid = (g_star[:, None] * EXPERTS_PER_GROUP + top_i).reshape(-1).astype(jnp.int32)
    tok = jnp.repeat(jnp.arange(T, dtype=jnp.int32), TOP_K)
    wflat = w_tok.reshape(-1)
    A = T * TOP_K
    n_blocks = -(-A // MOE_BLOCK) + N_EXPERTS
    P = n_blocks * MOE_BLOCK

    order = jnp.argsort(eid)
    s_eid = eid[order]
    counts = jnp.zeros((N_EXPERTS,), jnp.int32).at[eid].add(1)
    starts = jnp.cumsum(counts) - counts
    pcounts = (counts + MOE_BLOCK - 1) // MOE_BLOCK * MOE_BLOCK
    pends = jnp.cumsum(pcounts)
    pstarts = pends - pcounts
    dest = pstarts[s_eid] + (jnp.arange(A, dtype=jnp.int32) - starts[s_eid])
    buf_tok = jnp.full((P,), T, jnp.int32).at[dest].set(tok[order])
    buf_w = jnp.zeros((P,), jnp.float32).at[dest].set(wflat[order])
    blk_e = jnp.minimum(jnp.searchsorted(pends, jnp.arange(n_blocks, dtype=jnp.int32) * MOE_BLOCK, side='right'),
                        N_EXPERTS - 1).astype(jnp.int32)

    h_pad = jnp.concatenate([h, jnp.zeros((1, D), h.dtype)], axis=0)
    xb = h_pad[buf_tok].reshape(n_blocks, MOE_BLOCK, D)

    def expert_block(args):
        xblk, e = args
        return (jax.nn.silu(xblk @ w1[e]) * (xblk @ w3[e])) @ w2[e]

    yb = lax.map(expert_block, (xb, blk_e)).reshape(P, D)
    out = jnp.zeros((T + 1, D), yb.dtype).at[buf_tok].add(yb * buf_w[:, None].astype(yb.dtype))
    return out[:T]


def setup_inputs(seed: int = 0) -> dict:
    key = jax.random.key(seed)
    ks = jax.random.split(key, 28)
    f32 = jnp.float32

    def nrm(k, shape, fan_in):
        return jax.random.normal(k, shape, f32) * (fan_in ** -0.5)

    def gain(k, n):
        return 1.0 + 0.05 * jax.random.normal(k, (DEPTH, n), f32)

    def small(k, shape):
        return 0.01 * jax.random.normal(k, shape, f32)

    u = jax.random.uniform(ks[17], (DEPTH, D_RG), f32, minval=0.9, maxval=0.999)
    s = u ** (1.0 / RG_C)
    rg_lambda = jnp.log(s) - jnp.log1p(-s)
    return {
        "x": jax.random.normal(ks[0], (BATCH, SEQ, D_MODEL), f32),
        "c": jax.random.normal(ks[1], (BATCH, D_MODEL), f32),
        "w_ada": 0.5 * nrm(ks[2], (DEPTH, D_MODEL, 6 * D_MODEL), D_MODEL),
        "b_ada": small(ks[3], (DEPTH, 6 * D_MODEL)),
        "g_pre_mix": gain(ks[4], D_MODEL),
        "g_post_mix": gain(ks[5], D_MODEL),
        "g_pre_ffn": gain(ks[6], D_MODEL),
        "g_post_ffn": gain(ks[7], D_MODEL),
        "w_in": nrm(ks[8], (DEPTH, D_MODEL, D_IN), D_MODEL),
        "hgrn_gamma": jax.random.normal(ks[9], (DEPTH + 1, D_HGRN), f32),
        "hgrn_norm_g": gain(ks[10], D_HGRN),
        "conv_w": nrm(ks[11], (DEPTH, CONV_WIDTH, D_RG), CONV_WIDTH),
        "conv_b": small(ks[12], (DEPTH, D_RG)),
        "rg_w_r": nrm(ks[13], (DEPTH, RG_BLOCKS, RG_BLOCK_DIM, RG_BLOCK_DIM), RG_BLOCK_DIM),
        "rg_b_r": small(ks[14], (DEPTH, D_RG)),
        "rg_w_i": nrm(ks[15], (DEPTH, RG_BLOCKS, RG_BLOCK_DIM, RG_BLOCK_DIM), RG_BLOCK_DIM),
        "rg_b_i": small(ks[16], (DEPTH, D_RG)),
        "rg_lambda": rg_lambda,
        "rg_norm_g": gain(ks[18], D_RG),
        "w_out": nrm(ks[19], (DEPTH, D_MIX, D_MODEL), D_MIX),
        "w_router_group": nrm(ks[20], (DEPTH, D_MODEL, N_GROUPS), D_MODEL),
        "b_router_group": small(ks[21], (DEPTH, N_GROUPS)),
        "w_router_expert": nrm(ks[22], (DEPTH, D_MODEL, N_EXPERTS), D_MODEL),
        "b_router_expert": small(ks[23], (DEPTH, N_EXPERTS)),
        "w1": nrm(ks[24], (DEPTH, N_EXPERTS, D_MODEL, D_EXPERT), D_MODEL),
        "w3": nrm(ks[25], (DEPTH, N_EXPERTS, D_MODEL, D_EXPERT), D_MODEL),
        "w2": nrm(ks[26], (DEPTH, N_EXPERTS, D_EXPERT, D_MODEL), D_EXPERT),
    }


def reference(x, c, w_ada, b_ada, g_pre_mix, g_post_mix, g_pre_ffn, g_post_ffn, w_in, hgrn_gamma,
              hgrn_norm_g, conv_w, conv_b, rg_w_r, rg_b_r, rg_w_i, rg_b_i, rg_lambda, rg_norm_g, w_out,
              w_router_group, b_router_group, w_router_expert, b_router_expert, w1, w3, w2):
    B, S, D = x.shape
    lb_table = jnp.cumsum(jax.nn.softmax(hgrn_gamma.astype(jnp.float32), axis=0), axis=0)
    for l in range(DEPTH):
        mod = jax.nn.silu(c) @ w_ada[l] + b_ada[l]
        sh1, sc1, gt1, sh2, sc2, gt2 = [m[:, None, :] for m in jnp.split(mod, 6, axis=-1)]

        h = rmsnorm(x, g_pre_mix[l]) * (1.0 + sc1) + sh1
        proj = h @ w_in[l]
        q, fz, iv, gz, xr, yr = jnp.split(proj, [D_HGRN, 2 * D_HGRN, 3 * D_HGRN, 4 * D_HGRN, 4 * D_HGRN + D_RG], axis=-1)

        lb = lb_table[l]
        fz32 = fz.astype(jnp.float32)
        f = lb + (1.0 - lb) * jax.nn.sigmoid(fz32)
        k_in = (1.0 - lb) * jax.nn.sigmoid(-fz32)
        hs = (B, S, HGRN_HEADS, HGRN_HEAD_DIM)
        o_a = hgrn2_chunkwise(jax.nn.silu(q.astype(jnp.float32)).reshape(hs), k_in.reshape(hs),
                              iv.astype(jnp.float32).reshape(hs), jnp.log(f).reshape(hs))
        o_a = rmsnorm(o_a, hgrn_norm_g[l].reshape(HGRN_HEADS, HGRN_HEAD_DIM)).reshape(B, S, D_HGRN)
        o_a = (o_a * jax.nn.silu(gz.astype(jnp.float32))).astype(x.dtype)

        xc = causal_dwconv(xr, conv_w[l], conv_b[l]).astype(jnp.float32)
        hr = rg_lru(xc, rg_w_r[l], rg_b_r[l], rg_w_i[l], rg_b_i[l], rg_lambda[l])
        o_b = rmsnorm(hr * jax.nn.gelu(yr.astype(jnp.float32)), rg_norm_g[l]).astype(x.dtype)

        mix = jnp.concatenate([o_a, o_b], axis=-1) @ w_out[l]
        x = x + gt1 * rmsnorm(mix, g_post_mix[l])

        h = rmsnorm(x, g_pre_ffn[l]) * (1.0 + sc2) + sh2
        y = hier_moe(h.reshape(B * S, D), w_router_group[l], b_router_group[l], w_router_expert[l],
                     b_router_expert[l], w1[l], w3[l], w2[l]).reshape(B, S, D)
        x = x + gt2 * rmsnorm(y, g_post_ffn[l])
    return x
```

```python
import functools

import numpy as np
import jax
import jax.numpy as jnp
from jax import lax
from jax.experimental import pallas as pl
from jax.experimental.pallas import tpu as pltpu

F32 = jnp.float32
BF16 = jnp.bfloat16

LANES = 128
SUBLANES = 8
VMEM_LIMIT_BYTES = 56 * 1024 * 1024

EPS = 1e-6
HEAD_DIM = 128
CHUNK = 128
SUB = 8
RG_C = 8.0
RG_BLOCKS = 8
CONV_WIDTH = 4
N_GROUPS = 4
EXPERTS_PER_GROUP = 8
N_EXPERTS = N_GROUPS * EXPERTS_PER_GROUP
TOP_K = 2
MOE_BLOCK = 256
TILE_S = 256
TILE_T = 256
GROUP_LANE0 = N_EXPERTS


def _levels(chunk):
    out, b = [], chunk // 2
    while b >= SUB:
        out.append(b)
        b //= 2
    return out


def _hier_constants(chunk):
    t = np.arange(chunk)
    low = (t[None, :] <= t[:, None]).astype(np.float32)
    mats, masks = [low], []
    for b in _levels(chunk):
        ref = (t // (2 * b)) * 2 * b + b - 1
        mats.append(low - low[ref])
        same = (t[:, None] // (2 * b)) == (t[None, :] // (2 * b))
        upper = (t[:, None] % (2 * b)) >= b
        lower = (t[None, :] % (2 * b)) < b
        masks.append((same & upper & lower).astype(np.float32))
    return np.concatenate(mats, 0), np.stack(masks, 0)


def _rms(v, g):
    return v * lax.rsqrt(jnp.mean(v * v, axis=-1, keepdims=True) + EPS) * g


def _sigmoid(v):
    return 1.0 / (1.0 + jnp.exp(-v))


def _dot(a, b):
    return jnp.dot(a, b, preferred_element_type=F32)


def _dot_nt(a, b):
    return lax.dot_general(a, b, (((1,), (1,)), ((), ())), preferred_element_type=F32)


def _split3(v):
    hi = v.astype(BF16)
    r1 = v - hi.astype(F32)
    mid = r1.astype(BF16)
    lo = (r1 - mid.astype(F32)).astype(BF16)
    return hi, mid, lo


def _ada_kernel(c_ref, w_ref, b_ref, o_ref):
    c = c_ref[...]
    sc = c * _sigmoid(c)
    s_hi, s_mid, _ = _split3(sc)
    w = w_ref[...]
    w_hi, w_mid, _ = _split3(w)
    acc = _dot(s_hi, w_hi) + (_dot(s_mid, w_hi) + _dot(s_hi, w_mid))
    o_ref[...] = acc + b_ref[...]


def _ada(c8, w_ada, b_ada):
    d, n = w_ada.shape
    tn = 512
    return pl.pallas_call(
        _ada_kernel,
        grid=(n // tn,),
        in_specs=[pl.BlockSpec((SUBLANES, d), lambda j: (0, 0)),
                  pl.BlockSpec((d, tn), lambda j: (0, j)),
                  pl.BlockSpec((1, tn), lambda j: (0, j))],
        out_specs=pl.BlockSpec((SUBLANES, tn), lambda j: (0, j)),
        out_shape=jax.ShapeDtypeStruct((SUBLANES, n), F32),
        compiler_params=pltpu.CompilerParams(dimension_semantics=("arbitrary",)),
        name="ada_mod",
    )(c8, w_ada, b_ada)


_P_GAMMA0, _P_GAMMA1, _P_HNORM, _P_CW0, _P_CB, _P_BR, _P_BI, _P_LAM, _P_RNORM = 0, 1, 2, 3, 7, 8, 9, 10, 11
_P_GPRE, _P_GPOST, _P_GFFN = 0, 1, 2
_R_E0, _R_E1, _R_W0, _R_W1, _R_RANK0, _R_RANK1 = 0, 1, 2, 3, 4, 5


def _mix_kernel(x_ref, mod_ref, p1024_ref, p512_ref, w_in_ref, w_ri_ref, w_out_ref, w_rt_ref, b_rt_ref,
                hmat_ref, hmask_ref, ltri_ref,
                x1_ref, h2_ref, route_ref, cnt_ref,
                proj_s, qs_s, kk_s, ea_s, od_s, o_s, st_s, rgc_s, convc_s, cntc_s):
    ts = x_ref.shape[1]
    dh = qs_s.shape[1]
    dr = convc_s.shape[1]
    nh = dh // HEAD_DIM
    n_lv = hmask_ref.shape[0]
    si = pl.program_id(1)

    @pl.when(si == 0)
    def _():
        st_s[...] = jnp.zeros_like(st_s)
        rgc_s[...] = jnp.zeros_like(rgc_s)
        convc_s[...] = jnp.zeros_like(convc_s)

    @pl.when(jnp.logical_and(pl.program_id(0) == 0, si == 0))
    def _():
        cntc_s[...] = jnp.zeros_like(cntc_s)

    x = x_ref[0]
    sh1, sc1, gt1 = mod_ref[0, 0:1, :], mod_ref[0, 1:2, :], mod_ref[0, 2:3, :]
    sh2, sc2 = mod_ref[0, 3:4, :], mod_ref[0, 4:5, :]

    h = _rms(x, p1024_ref[_P_GPRE:_P_GPRE + 1, :]) * (1.0 + sc1) + sh1
    proj_s[...] = _dot(h.astype(BF16), w_in_ref[...])

    g0 = p512_ref[_P_GAMMA0:_P_GAMMA0 + 1, :]
    g1 = p512_ref[_P_GAMMA1:_P_GAMMA1 + 1, :]
    gm = jnp.maximum(g0, g1)
    e0 = jnp.exp(g0 - gm)
    lb = e0 / (e0 + jnp.exp(g1 - gm))
    hnorm = p512_ref[_P_HNORM:_P_HNORM + 1, :]

    for c in range(ts // CHUNK):
        r0 = c * CHUNK
        q = proj_s[r0:r0 + CHUNK, 0:dh]
        fz = proj_s[r0:r0 + CHUNK, dh:2 * dh]
        qs_s[...] = q * _sigmoid(q)
        ez = jnp.exp(-jnp.abs(fz))
        inv = 1.0 / (1.0 + ez)
        pos = fz >= 0.0
        sg = jnp.where(pos, inv, ez * inv)
        sn = jnp.where(pos, ez * inv, inv)
        f = lb + (1.0 - lb) * sg
        kk_s[...] = (1.0 - lb) * sn
        g = jnp.log(f)
        ghi, gmid, glo = _split3(g)
        ea3 = _dot(hmat_ref[...], jnp.concatenate([ghi, gmid, glo], axis=1))
        ea_s[...] = (ea3[:, 0:dh] + ea3[:, dh:2 * dh]) + ea3[:, 2 * dh:3 * dh]

        for hd in range(nh):
            c0 = hd * HEAD_DIM
            qh = qs_s[:, c0:c0 + HEAD_DIM]
            kh = kk_s[:, c0:c0 + HEAD_DIM]
            vh = proj_s[r0:r0 + CHUNK, 2 * dh + c0:2 * dh + c0 + HEAD_DIM]
            b = ea_s[0:CHUNK, c0:c0 + HEAD_DIM]
            s_acc = jnp.zeros((CHUNK, CHUNK), F32)
            for lv in range(n_lv):
                e_lv = ea_s[(lv + 1) * CHUNK:(lv + 2) * CHUNK, c0:c0 + HEAD_DIM]
                w_lv = jnp.exp(jnp.minimum(e_lv, -e_lv))
                s_acc = s_acc + hmask_ref[lv] * _dot_nt((qh * w_lv).astype(BF16), (kh * w_lv).astype(BF16))
            o = _dot(s_acc.astype(BF16), vh.astype(BF16))
            st_t = st_s[hd]
            o = o + _dot_nt((qh * jnp.exp(b)).astype(BF16), st_t.astype(BF16))
            b_last = ea_s[CHUNK - 1:CHUNK, c0:c0 + HEAD_DIM]
            kd = (kh * jnp.exp(b_last - b)).astype(BF16)
            st_s[hd] = st_t * jnp.exp(b_last) + _dot(vh.T.astype(BF16), kd)
            od_s[:, c0:c0 + HEAD_DIM] = o

        row_i = lax.broadcasted_iota(jnp.int32, (SUB, dh), 0)

        def sub_block(r, carry):
            r8 = pl.multiple_of(r * SUB, SUB)
            qr = qs_s[pl.ds(r8, SUB), :]
            br = ea_s[pl.ds(r8, SUB), :]
            acc = od_s[pl.ds(r8, SUB), :]
            for s in range(SUB):
                bs = ea_s[pl.ds(r8 + s, 1), :]
                ks = kk_s[pl.ds(r8 + s, 1), :]
                vs = proj_s[pl.ds(r0 + r8 + s, 1), 2 * dh:3 * dh]
                a = qr * ks * jnp.exp(jnp.minimum(br - bs, 0.0))
                a = jnp.where(row_i >= s, a, 0.0)
                parts = []
                for hd in range(nh):
                    c0 = hd * HEAD_DIM
                    sc = jnp.sum(a[:, c0:c0 + HEAD_DIM], axis=1, keepdims=True)
                    parts.append(sc * vs[:, c0:c0 + HEAD_DIM])
                acc = acc + jnp.concatenate(parts, axis=1)
            od_s[pl.ds(r8, SUB), :] = acc
            return carry

        lax.fori_loop(0, CHUNK // SUB, sub_block, 0)

        gz = proj_s[r0:r0 + CHUNK, 3 * dh:4 * dh]
        sgz = gz * _sigmoid(gz)
        outs = []
        for hd in range(nh):
            c0 = hd * HEAD_DIM
            outs.append(_rms(od_s[:, c0:c0 + HEAD_DIM], hnorm[:, c0:c0 + HEAD_DIM]))
        o_s[r0:r0 + CHUNK, 0:dh] = (jnp.concatenate(outs, axis=1) * sgz).astype(BF16)

    xr = proj_s[:, 4 * dh:4 * dh + dr]
    yr = proj_s[:, 4 * dh + dr:4 * dh + 2 * dr]
    ext = jnp.concatenate([convc_s[...], xr], axis=0)
    xc = p512_ref[_P_CW0 + CONV_WIDTH - 1:_P_CW0 + CONV_WIDTH, :] * xr + p512_ref[_P_CB:_P_CB + 1, :]
    for k in range(1, CONV_WIDTH):
        wk = p512_ref[_P_CW0 + CONV_WIDTH - 1 - k:_P_CW0 + CONV_WIDTH - k, :]
        xc = xc + wk * pltpu.roll(ext, k, 0)[SUBLANES:, :]
    convc_s[...] = xr[ts - SUBLANES:, :]

    gates = _dot(xc.astype(BF16), w_ri_ref[...])
    r_g = _sigmoid(gates[:, 0:dr] + p512_ref[_P_BR:_P_BR + 1, :])
    i_g = _sigmoid(gates[:, dr:2 * dr] + p512_ref[_P_BI:_P_BI + 1, :])
    lam = p512_ref[_P_LAM:_P_LAM + 1, :]
    log_sig = jnp.minimum(lam, 0.0) - jnp.log(1.0 + jnp.exp(-jnp.abs(lam)))
    a_t = jnp.exp(RG_C * r_g * log_sig)
    u_t = jnp.sqrt(1.0 - a_t * a_t) * (i_g * xc)
    row_t = lax.broadcasted_iota(jnp.int32, (ts, dr), 0)
    d = 1
    while d < ts:
        keep = row_t >= d
        a_prev = jnp.where(keep, pltpu.roll(a_t, d, 0), 1.0)
        u_prev = jnp.where(keep, pltpu.roll(u_t, d, 0), 0.0)
        u_t = a_t * u_prev + u_t
        a_t = a_t * a_prev
        d *= 2
    h_rg = a_t * rgc_s[SUBLANES - 1:SUBLANES, :] + u_t
    rgc_s[...] = h_rg[ts - SUBLANES:, :]
    gelu = 0.5 * yr * (1.0 + jnp.tanh(0.7978845608028654 * (yr + 0.044715 * (yr * yr * yr))))
    o_s[:, dh:dh + dr] = _rms(h_rg * gelu, p512_ref[_P_RNORM:_P_RNORM + 1, :]).astype(BF16)

    mix = _dot(o_s[...], w_out_ref[...])
    x1 = x + gt1 * _rms(mix, p1024_ref[_P_GPOST:_P_GPOST + 1, :])
    x1_ref[0] = x1
    h2 = _rms(x1, p1024_ref[_P_GFFN:_P_GFFN + 1, :]) * (1.0 + sc2) + sh2
    h2_ref[0] = h2

    h_hi, h_mid, _ = _split3(h2)
    w_hi, w_mid, _ = _split3(w_rt_ref[...])
    lg = _dot(h_hi, w_hi) + (_dot(h_mid, w_hi) + _dot(h_hi, w_mid)) + b_rt_ref[...]
    lane = lax.broadcasted_iota(jnp.int32, (ts, LANES), 1)
    lane_f = lane.astype(F32)
    neg = jnp.float32(-jnp.inf)
    big = jnp.float32(2 * LANES)
    is_g = jnp.logical_and(lane >= GROUP_LANE0, lane < GROUP_LANE0 + N_GROUPS)
    gl = jnp.where(is_g, lg, neg)
    g_max = jnp.max(gl, axis=1, keepdims=True)
    g_star = jnp.min(jnp.where(gl == g_max, lane_f, big), axis=1, keepdims=True) - float(GROUP_LANE0)
    gate_g = 1.0 / jnp.sum(jnp.exp(gl - g_max), axis=1, keepdims=True)
    grp_f = jnp.floor(lane_f * (1.0 / EXPERTS_PER_GROUP))
    el = jnp.where(jnp.logical_and(lane < N_EXPERTS, grp_f == g_star), lg, neg)
    m1 = jnp.max(el, axis=1, keepdims=True)
    i1 = jnp.min(jnp.where(el == m1, lane_f, big), axis=1, keepdims=True)
    el2 = jnp.where(lane_f == i1, neg, el)
    m2 = jnp.max(el2, axis=1, keepdims=True)
    i2 = jnp.min(jnp.where(el2 == m2, lane_f, big), axis=1, keepdims=True)
    e21 = jnp.exp(m2 - m1)
    w0 = gate_g / (1.0 + e21)
    w1 = gate_g * e21 / (1.0 + e21)

    hit0 = lane_f == i1
    hit1 = lane_f == i2
    oh = jnp.where(jnp.logical_or(hit0, hit1), 1.0, 0.0)
    before = _dot(ltri_ref[...], oh.astype(BF16)) + cntc_s[0:1, :]
    rank0 = jnp.sum(jnp.where(hit0, before, 0.0), axis=1, keepdims=True)
    rank1 = jnp.sum(jnp.where(hit1, before, 0.0), axis=1, keepdims=True)
    cntc_s[...] = cntc_s[...] + jnp.sum(oh, axis=0, keepdims=True)
    cnt_ref[...] = cntc_s[...]

    route = jnp.where(lane == _R_E0, i1, 0.0)
    route = jnp.where(lane == _R_E1, i2, route)
    route = jnp.where(lane == _R_W0, w0, route)
    route = jnp.where(lane == _R_W1, w1, route)
    route = jnp.where(lane == _R_RANK0, rank0, route)
    route = jnp.where(lane == _R_RANK1, rank1, route)
    route_ref[...] = route


def _mix(x, mod, p1024, p512, w_in, w_ri, w_out, w_rt, b_rt):
    bsz, seq, d = x.shape
    ts = TILE_S
    dh = p512.shape[1]
    dr = p512.shape[1]
    nh = dh // HEAD_DIM
    n_s = seq // ts
    hmat_np, hmask_np = _hier_constants(CHUNK)
    hmat = jnp.asarray(hmat_np, BF16)
    hmask = jnp.asarray(hmask_np, F32)
    tt = np.arange(ts)
    ltri = jnp.asarray((tt[None, :] < tt[:, None]).astype(np.float32), BF16)

    def const(shape):
        return pl.BlockSpec(shape, lambda b, s: (0,) * len(shape))

    return pl.pallas_call(
        _mix_kernel,
        grid=(bsz, n_s),
        in_specs=[
            pl.BlockSpec((1, ts, d), lambda b, s: (b, s, 0)),
            pl.BlockSpec((1, mod.shape[1], d), lambda b, s: (b, 0, 0)),
            const(p1024.shape), const(p512.shape), const(w_in.shape), const(w_ri.shape),
            const(w_out.shape), const(w_rt.shape), const(b_rt.shape),
            const(hmat.shape), const(hmask.shape), const(ltri.shape),
        ],
        out_specs=[
            pl.BlockSpec((1, ts, d), lambda b, s: (b, s, 0)),
            pl.BlockSpec((1, ts, d), lambda b, s: (b, s, 0)),
            pl.BlockSpec((ts, LANES), lambda b, s: (b * n_s + s, 0)),
            pl.BlockSpec((SUBLANES, LANES), lambda b, s: (0, 0)),
        ],
        out_shape=[
            jax.ShapeDtypeStruct((bsz, seq, d), F32),
            jax.ShapeDtypeStruct((bsz, seq, d), F32),
            jax.ShapeDtypeStruct((bsz * seq, LANES), F32),
            jax.ShapeDtypeStruct((SUBLANES, LANES), F32),
        ],
        scratch_shapes=[
            pltpu.VMEM((ts, w_in.shape[1]), F32),
            pltpu.VMEM((CHUNK, dh), F32),
            pltpu.VMEM((CHUNK, dh), F32),
            pltpu.VMEM((hmat.shape[0], dh), F32),
            pltpu.VMEM((CHUNK, dh), F32),
            pltpu.VMEM((ts, dh + dr), BF16),
            pltpu.VMEM((nh, HEAD_DIM, HEAD_DIM), F32),
            pltpu.VMEM((SUBLANES, dr), F32),
            pltpu.VMEM((SUBLANES, dr), F32),
            pltpu.VMEM((SUBLANES, LANES), F32),
        ],
        compiler_params=pltpu.CompilerParams(
            dimension_semantics=("arbitrary", "arbitrary"), vmem_limit_bytes=VMEM_LIMIT_BYTES),
        name="mix_route",
    )(x, mod, p1024, p512, w_in, w_ri, w_out, w_rt, b_rt, hmat, hmask, ltri)


def _dispatch_kernel(dest_ref, h2_ref, xb_in_ref, xb_ref, sem):
    del xb_in_ref
    tt = h2_ref.shape[0]

    def row_copy(t, slot):
        return pltpu.make_async_copy(h2_ref.at[pl.ds(t, 1)], xb_ref.at[pl.ds(slot, 1)], sem)

    def issue(t, carry):
        row_copy(t, dest_ref[0, 0, 2 * t]).start()
        row_copy(t, dest_ref[0, 0, 2 * t + 1]).start()
        return carry

    lax.fori_loop(0, tt, issue, 0, unroll=8)

    def drain(t, carry):
        row_copy(0, 0).wait()
        row_copy(0, 0).wait()
        return carry

    lax.fori_loop(0, tt, drain, 0, unroll=8)


def _dispatch(dest, h2, xb0):
    n_tok, d = h2.shape
    tt = TILE_T
    return pl.pallas_call(
        _dispatch_kernel,
        grid=(n_tok // tt,),
        in_specs=[
            pl.BlockSpec((1, 1, TOP_K * tt), lambda i: (i, 0, 0), memory_space=pltpu.SMEM),
            pl.BlockSpec((tt, d), lambda i: (i, 0)),
            pl.BlockSpec(memory_space=pl.ANY),
        ],
        out_specs=pl.BlockSpec(memory_space=pl.ANY),
        out_shape=jax.ShapeDtypeStruct(xb0.shape, xb0.dtype),
        scratch_shapes=[pltpu.SemaphoreType.DMA(())],
        input_output_aliases={2: 0},
        compiler_params=pltpu.CompilerParams(dimension_semantics=("arbitrary",)),
        name="dispatch_rows",
    )(dest, h2, xb0)


def _ffn_kernel(blk_e_ref, nblk_ref, xb_ref, w1_ref, w3_ref, w2_ref, yb_ref, w1b, w3b, w2b):
    b = pl.program_id(0)
    prev = blk_e_ref[jnp.maximum(b - 1, 0)]
    fresh = jnp.logical_or(b == 0, blk_e_ref[b] != prev)
    live = b < nblk_ref[0]

    @pl.when(jnp.logical_and(fresh, live))
    def _():
        w1b[...] = w1_ref[0].astype(BF16)
        w3b[...] = w3_ref[0].astype(BF16)
        w2b[...] = w2_ref[0].astype(BF16)

    @pl.when(live)
    def _():
        xv = xb_ref[...].astype(BF16)
        a = _dot(xv, w1b[...])
        g = _dot(xv, w3b[...])
        hact = (a * _sigmoid(a) * g).astype(BF16)
        yb_ref[...] = _dot(hact, w2b[...])

    @pl.when(jnp.logical_not(live))
    def _():
        yb_ref[...] = jnp.zeros_like(yb_ref)


def _ffn(blk_e, nblk, xb, w1, w3, w2):
    n_slots, d = xb.shape
    n_blocks = n_slots // MOE_BLOCK
    de = w1.shape[2]

    def x_map(b, be, nb):
        return (jnp.minimum(b, nb[0] - 1), 0)

    def w_map(b, be, nb):
        return (be[b], 0, 0)

    grid_spec = pltpu.PrefetchScalarGridSpec(
        num_scalar_prefetch=2,
        grid=(n_blocks,),
        in_specs=[
            pl.BlockSpec((MOE_BLOCK, d), x_map),
            pl.BlockSpec((1, d, de), w_map),
            pl.BlockSpec((1, d, de), w_map),
            pl.BlockSpec((1, de, d), w_map),
        ],
        out_specs=pl.BlockSpec((MOE_BLOCK, d), lambda b, be, nb: (b, 0)),
        scratch_shapes=[pltpu.VMEM((d, de), BF16), pltpu.VMEM((d, de), BF16), pltpu.VMEM((de, d), BF16)],
    )
    return pl.pallas_call(
        _ffn_kernel,
        grid_spec=grid_spec,
        out_shape=jax.ShapeDtypeStruct((n_slots, d), F32),
        compiler_params=pltpu.CompilerParams(
            dimension_semantics=("arbitrary",), vmem_limit_bytes=VMEM_LIMIT_BYTES),
        name="expert_ffn",
    )(blk_e, nblk, xb, w1, w3, w2)


def _combine_kernel(dest_ref, yb_ref, route_ref, x1_ref, mod_ref, g_ref, out_ref, ybuf, sem):
    tt = x1_ref.shape[0]

    def row_copy(slot, k, t):
        return pltpu.make_async_copy(yb_ref.at[pl.ds(slot, 1)], ybuf.at[k, pl.ds(t, 1)], sem)

    def issue(t, carry):
        row_copy(dest_ref[0, 0, 2 * t], 0, t).start()
        row_copy(dest_ref[0, 0, 2 * t + 1], 1, t).start()
        return carry

    lax.fori_loop(0, tt, issue, 0, unroll=8)

    def drain(t, carry):
        row_copy(0, 0, 0).wait()
        row_copy(0, 0, 0).wait()
        return carry

    lax.fori_loop(0, tt, drain, 0, unroll=8)

    w0 = route_ref[:, _R_W0:_R_W0 + 1]
    w1 = route_ref[:, _R_W1:_R_W1 + 1]
    y = ybuf[0] * w0 + ybuf[1] * w1
    gt2 = mod_ref[0, 5:6, :]
    out_ref[...] = x1_ref[...] + gt2 * _rms(y, g_ref[...])


def _combine(dest, yb, route, x1, mod, g_post_ffn, seq):
    n_tok, d = x1.shape
    tt = TILE_T
    per_b = seq // tt
    return pl.pallas_call(
        _combine_kernel,
        grid=(n_tok // tt,),
        in_specs=[
            pl.BlockSpec((1, 1, TOP_K * tt), lambda i: (i, 0, 0), memory_space=pltpu.SMEM),
            pl.BlockSpec(memory_space=pl.ANY),
            pl.BlockSpec((tt, LANES), lambda i: (i, 0)),
            pl.BlockSpec((tt, d), lambda i: (i, 0)),
            pl.BlockSpec((1, mod.shape[1], d), lambda i: (i // per_b, 0, 0)),
            pl.BlockSpec((1, d), lambda i: (0, 0)),
        ],
        out_specs=pl.BlockSpec((tt, d), lambda i: (i, 0)),
        out_shape=jax.ShapeDtypeStruct((n_tok, d), F32),
        scratch_shapes=[pltpu.VMEM((TOP_K, tt, d), F32), pltpu.SemaphoreType.DMA(())],
        compiler_params=pltpu.CompilerParams(dimension_semantics=("arbitrary",)),
        name="combine_rows",
    )(dest, yb, route, x1, mod, g_post_ffn)


def _block_diag(w):
    nb, bi, bo = w.shape
    eye = jnp.eye(nb, dtype=w.dtype)
    return (w[:, :, None, :] * eye[:, None, :, None]).reshape(nb * bi, nb * bo)


def kernel(x, c, w_ada, b_ada, g_pre_mix, g_post_mix, g_pre_ffn, g_post_ffn, w_in, hgrn_gamma, hgrn_norm_g, conv_w, conv_b, rg_w_r, rg_b_r, rg_w_i, rg_b_i, rg_lambda, rg_norm_g, w_out, w_router_group, b_router_group, w_router_expert, b_router_expert, w1, w3, w2):
    bsz, seq, d = x.shape
    depth = w_ada.shape[0]
    assert depth == 1 and hgrn_gamma.shape[0] == 2
    dh = hgrn_norm_g.shape[1]
    dr = rg_norm_g.shape[1]
    assert dh == dr and dh % HEAD_DIM == 0 and seq % TILE_S == 0 and TILE_S % CHUNK == 0
    assert w_router_expert.shape[2] == N_EXPERTS and w_router_group.shape[2] == N_GROUPS
    n_tok = bsz * seq

    c8 = jnp.pad(c, ((0, SUBLANES - bsz), (0, 0)))
    p512 = jnp.concatenate([
        hgrn_gamma, hgrn_norm_g, conv_w[0], conv_b, rg_b_r, rg_b_i, rg_lambda, rg_norm_g,
        jnp.zeros((4, dh), F32)], axis=0)
    p1024 = jnp.concatenate([g_pre_mix, g_post_mix, g_pre_ffn, jnp.zeros((5, d), F32)], axis=0)
    w_ri = jnp.concatenate([_block_diag(rg_w_r[0]), _block_diag(rg_w_i[0])], axis=1).astype(BF16)
    w_rt = jnp.concatenate([w_router_expert[0], w_router_group[0],
                            jnp.zeros((d, LANES - N_EXPERTS - N_GROUPS), F32)], axis=1)
    b_rt = jnp.concatenate([b_router_expert[0], b_router_group[0],
                            jnp.zeros((LANES - N_EXPERTS - N_GROUPS,), F32)])[None, :]

    mod = _ada(c8, w_ada[0], b_ada[0][None, :])[:bsz].reshape(bsz, 6, d)

    x1, h2, route, cnt = _mix(x, mod, p1024, p512, w_in[0].astype(BF16), w_ri, w_out[0].astype(BF16), w_rt, b_rt)

    counts = cnt[0, :N_EXPERTS].astype(jnp.int32)
    pcounts = (counts + MOE_BLOCK - 1) // MOE_BLOCK * MOE_BLOCK
    pends = jnp.cumsum(pcounts)
    pstarts = pends - pcounts
    n_blocks = -(-(n_tok * TOP_K) // MOE_BLOCK) + N_EXPERTS
    blk_e = jnp.minimum(jnp.searchsorted(pends, jnp.arange(n_blocks, dtype=jnp.int32) * MOE_BLOCK, side='right'),
                        N_EXPERTS - 1).astype(jnp.int32)
    nblk = (pends[-1:] // MOE_BLOCK).astype(jnp.int32)
    eids = route[:, _R_E0:_R_E1 + 1].astype(jnp.int32)
    ranks = route[:, _R_RANK0:_R_RANK1 + 1].astype(jnp.int32)
    dest = (pstarts[eids] + ranks).reshape(n_tok // TILE_T, 1, TOP_K * TILE_T)

    xb0 = jnp.zeros((n_blocks * MOE_BLOCK, d), F32)
    xb = _dispatch(dest, h2.reshape(n_tok, d), xb0)
    yb = _ffn(blk_e, nblk, xb, w1[0], w3[0], w2[0])
    out = _combine(dest, yb, route, x1.reshape(n_tok, d), mod, g_post_ffn, seq)
    return out.reshape(bsz, seq, d)
```

```python
import functools

import numpy as np
import jax
import jax.numpy as jnp
from jax import lax
from jax.experimental import pallas as pl
from jax.experimental.pallas import tpu as pltpu

F32 = jnp.float32
BF16 = jnp.bfloat16

LANES = 128
SUBLANES = 8
VMEM_LIMIT_BYTES = 56 * 1024 * 1024

EPS = 1e-6
HEAD_DIM = 128
CHUNK = 128
SUB = 8
RG_C = 8.0
LOG2E = 1.4426950408889634
TINY = 1e-30
RG_BLOCKS = 8
CONV_WIDTH = 4
N_GROUPS = 4
EXPERTS_PER_GROUP = 8
N_EXPERTS = N_GROUPS * EXPERTS_PER_GROUP
TOP_K = 2
MOE_BLOCK = 256
TILE_S = 256
TILE_T = 256
GROUP_LANE0 = N_EXPERTS


def _levels(chunk):
    out, b = [], chunk // 2
    while b >= SUB:
        out.append(b)
        b //= 2
    return out


def _hier_constants(chunk):
    t = np.arange(chunk)
    low = (t[None, :] <= t[:, None]).astype(np.float32)
    mats, masks = [low], []
    for b in _levels(chunk):
        ref = (t // (2 * b)) * 2 * b + b - 1
        mats.append(low - low[ref])
        same = (t[:, None] // (2 * b)) == (t[None, :] // (2 * b))
        upper = (t[:, None] % (2 * b)) >= b
        lower = (t[None, :] % (2 * b)) < b
        masks.append((same & upper & lower).astype(np.float32))
    return np.concatenate(mats, 0), np.stack(masks, 0)


def _rms(v, g):
    return v * lax.rsqrt(jnp.mean(v * v, axis=-1, keepdims=True) + EPS) * g


def _sigmoid(v):
    return 1.0 / (1.0 + jnp.exp(-v))


def _dot(a, b):
    return jnp.dot(a, b, preferred_element_type=F32)


def _dot_nt(a, b):
    return lax.dot_general(a, b, (((1,), (1,)), ((), ())), preferred_element_type=F32)


def _split3(v):
    hi = v.astype(BF16)
    r1 = v - hi.astype(F32)
    mid = r1.astype(BF16)
    lo = (r1 - mid.astype(F32)).astype(BF16)
    return hi, mid, lo


def _ada_kernel(c_ref, w_ref, b_ref, o_ref):
    c = c_ref[...]
    sc = c * _sigmoid(c)
    s_hi, s_mid, _ = _split3(sc)
    w = w_ref[...]
    w_hi, w_mid, _ = _split3(w)
    acc = _dot(s_hi, w_hi) + (_dot(s_mid, w_hi) + _dot(s_hi, w_mid))
    o_ref[...] = acc + b_ref[...]


def _ada(c8, w_ada, b_ada):
    d, n = w_ada.shape
    tn = 512
    return pl.pallas_call(
        _ada_kernel,
        grid=(n // tn,),
        in_specs=[pl.BlockSpec((SUBLANES, d), lambda j: (0, 0)),
                  pl.BlockSpec((d, tn), lambda j: (0, j)),
                  pl.BlockSpec((1, tn), lambda j: (0, j))],
        out_specs=pl.BlockSpec((SUBLANES, tn), lambda j: (0, j)),
        out_shape=jax.ShapeDtypeStruct((SUBLANES, n), F32),
        compiler_params=pltpu.CompilerParams(dimension_semantics=("arbitrary",)),
        name="ada_mod",
    )(c8, w_ada, b_ada)


_P_GAMMA0, _P_GAMMA1, _P_HNORM, _P_CW0, _P_CB, _P_BR, _P_BI, _P_LAM, _P_RNORM = 0, 1, 2, 3, 7, 8, 9, 10, 11
_P_GPRE, _P_GPOST, _P_GFFN = 0, 1, 2
_R_E0, _R_E1, _R_W0, _R_W1, _R_RANK0, _R_RANK1 = 0, 1, 2, 3, 4, 5


def _mix_kernel(x_ref, mod_ref, p1024_ref, p512_ref, w_in_ref, w_ri_ref, w_out_ref, w_rt_ref, b_rt_ref,
                hmat_ref, hmask_ref, ltri_ref,
                x1_ref, h2_ref, route_ref, cnt_ref,
                proj_s, qs_s, kk_s, ea_s, od_s, o_s, st_s, rgc_s, ext_s, cntc_s):
    ts = x_ref.shape[1]
    dh = qs_s.shape[1]
    dr = rgc_s.shape[1]
    nh = dh // HEAD_DIM
    n_lv = hmask_ref.shape[0]
    si = pl.program_id(1)

    @pl.when(si == 0)
    def _():
        st_s[...] = jnp.zeros_like(st_s)
        rgc_s[...] = jnp.zeros_like(rgc_s)
        ext_s[0:SUBLANES, :] = jnp.zeros((SUBLANES, dr), F32)

    @pl.when(jnp.logical_and(pl.program_id(0) == 0, si == 0))
    def _():
        cntc_s[...] = jnp.zeros_like(cntc_s)

    x = x_ref[0]
    sh1, sc1, gt1 = mod_ref[0, 0:1, :], mod_ref[0, 1:2, :], mod_ref[0, 2:3, :]
    sh2, sc2 = mod_ref[0, 3:4, :], mod_ref[0, 4:5, :]

    h = _rms(x, p1024_ref[_P_GPRE:_P_GPRE + 1, :] * (1.0 + sc1)) + sh1
    proj_s[...] = _dot(h.astype(BF16), w_in_ref[...])

    g0 = p512_ref[_P_GAMMA0:_P_GAMMA0 + 1, :]
    g1 = p512_ref[_P_GAMMA1:_P_GAMMA1 + 1, :]
    gm = jnp.maximum(g0, g1)
    e0 = jnp.exp(g0 - gm)
    lb = e0 / (e0 + jnp.exp(g1 - gm))
    hnorm = p512_ref[_P_HNORM:_P_HNORM + 1, :]

    for c in range(ts // CHUNK):
        r0 = c * CHUNK
        q = proj_s[r0:r0 + CHUNK, 0:dh]
        fz = proj_s[r0:r0 + CHUNK, dh:2 * dh]
        qs_s[...] = q * _sigmoid(q)
        ez = jnp.exp(-jnp.abs(fz))
        inv = 1.0 / (1.0 + ez)
        pos = fz >= 0.0
        sg = jnp.where(pos, inv, ez * inv)
        sn = jnp.where(pos, ez * inv, inv)
        f = lb + (1.0 - lb) * sg
        kk_s[...] = (1.0 - lb) * sn
        g = jnp.log(f) * LOG2E
        ghi, gmid, glo = _split3(g)
        ea3 = _dot(hmat_ref[...], jnp.concatenate([ghi, gmid, glo], axis=1))
        ea_s[...] = (ea3[:, 0:dh] + ea3[:, dh:2 * dh]) + ea3[:, 2 * dh:3 * dh]

        for hd in range(nh):
            c0 = hd * HEAD_DIM
            qh = qs_s[:, c0:c0 + HEAD_DIM]
            kh = kk_s[:, c0:c0 + HEAD_DIM]
            vh = proj_s[r0:r0 + CHUNK, 2 * dh + c0:2 * dh + c0 + HEAD_DIM]
            b = ea_s[0:CHUNK, c0:c0 + HEAD_DIM]
            s_acc = jnp.zeros((CHUNK, CHUNK), F32)
            for lv in range(n_lv):
                e_lv = ea_s[(lv + 1) * CHUNK:(lv + 2) * CHUNK, c0:c0 + HEAD_DIM]
                w_lv = jnp.exp2(jnp.minimum(e_lv, -e_lv))
                s_acc = s_acc + hmask_ref[lv] * _dot_nt((qh * w_lv).astype(BF16), (kh * w_lv).astype(BF16))
            o = _dot(s_acc.astype(BF16), vh.astype(BF16))
            st_t = st_s[hd]
            o = o + _dot_nt((qh * jnp.exp2(b)).astype(BF16), st_t.astype(BF16))
            b_last = ea_s[CHUNK - 1:CHUNK, c0:c0 + HEAD_DIM]
            kd = (kh * jnp.exp2(b_last - b)).astype(BF16)
            st_s[hd] = st_t * jnp.exp2(b_last) + _dot(vh.T.astype(BF16), kd)
            od_s[:, c0:c0 + HEAD_DIM] = o

        row_i = lax.broadcasted_iota(jnp.int32, (SUB, dh), 0)
        ones_w = jnp.ones((HEAD_DIM, HEAD_DIM), BF16)

        for r in range(CHUNK // SUB):
            r8 = r * SUB
            q8 = qs_s[pl.ds(r8, SUB), :]
            k8 = kk_s[pl.ds(r8, SUB), :]
            b8 = ea_s[pl.ds(r8, SUB), :]
            v8 = proj_s[pl.ds(r0 + r8, SUB), 2 * dh:3 * dh]
            tiles, vjs = [], []
            for j in range(SUB):
                if j == 0:
                    kj, bj, vj = k8, b8, v8
                else:
                    kj = pltpu.roll(k8, j, 0)
                    bj = pltpu.roll(b8, j, 0)
                    vj = jnp.where(row_i >= j, pltpu.roll(v8, j, 0), 0.0)
                a = q8 * kj * jnp.exp2(jnp.minimum(b8 - bj, 0.0))
                for hd in range(nh):
                    tiles.append(a[:, hd * HEAD_DIM:(hd + 1) * HEAD_DIM])
                vjs.append(vj)
            red = _dot(jnp.concatenate(tiles, axis=0).astype(BF16), ones_w)
            acc = od_s[pl.ds(r8, SUB), :]
            for j in range(SUB):
                rj = [red[(j * nh + hd) * SUB:(j * nh + hd + 1) * SUB, :] for hd in range(nh)]
                acc = acc + jnp.concatenate(rj, axis=1) * vjs[j]
            od_s[pl.ds(r8, SUB), :] = acc

        gz = proj_s[r0:r0 + CHUNK, 3 * dh:4 * dh]
        sgz = gz * _sigmoid(gz)
        outs = []
        for hd in range(nh):
            c0 = hd * HEAD_DIM
            outs.append(_rms(od_s[:, c0:c0 + HEAD_DIM], hnorm[:, c0:c0 + HEAD_DIM]))
        o_s[r0:r0 + CHUNK, 0:dh] = (jnp.concatenate(outs, axis=1) * sgz).astype(BF16)

    xr = proj_s[:, 4 * dh:4 * dh + dr]
    yr = proj_s[:, 4 * dh + dr:4 * dh + 2 * dr]
    ext_s[SUBLANES:, :] = xr
    xc = p512_ref[_P_CW0 + CONV_WIDTH - 1:_P_CW0 + CONV_WIDTH, :] * xr + p512_ref[_P_CB:_P_CB + 1, :]
    for k in range(1, CONV_WIDTH):
        wk = p512_ref[_P_CW0 + CONV_WIDTH - 1 - k:_P_CW0 + CONV_WIDTH - k, :]
        xc = xc + wk * ext_s[SUBLANES - k:SUBLANES - k + ts, :]
    ext_s[0:SUBLANES, :] = xr[ts - SUBLANES:, :]

    gates = _dot(xc.astype(BF16), w_ri_ref[...])
    r_g = _sigmoid(gates[:, 0:dr] + p512_ref[_P_BR:_P_BR + 1, :])
    i_g = _sigmoid(gates[:, dr:2 * dr] + p512_ref[_P_BI:_P_BI + 1, :])
    lam = p512_ref[_P_LAM:_P_LAM + 1, :]
    log_sig = jnp.minimum(lam, 0.0) - jnp.log(1.0 + jnp.exp(-jnp.abs(lam)))
    a_t = jnp.exp(RG_C * r_g * log_sig)
    y_t = 1.0 - a_t * a_t
    u_t = y_t * lax.rsqrt(jnp.maximum(y_t, TINY)) * (i_g * xc)
    n_tiles = ts // SUBLANES
    a3 = a_t.reshape(n_tiles, SUBLANES, dr)
    u3 = u_t.reshape(n_tiles, SUBLANES, dr)
    row3 = lax.broadcasted_iota(jnp.int32, (n_tiles, SUBLANES, dr), 1)
    d = 1
    while d < SUBLANES:
        keep = row3 >= d
        a_prev = jnp.where(keep, pltpu.roll(a3, d, 1), 1.0)
        u_prev = jnp.where(keep, pltpu.roll(u3, d, 1), 0.0)
        u3 = a3 * u_prev + u3
        a3 = a3 * a_prev
        d *= 2
    carry = rgc_s[SUBLANES - 1:SUBLANES, :]
    h_tiles = []
    for t in range(n_tiles):
        h_t = a3[t] * carry + u3[t]
        h_tiles.append(h_t)
        carry = h_t[SUBLANES - 1:SUBLANES, :]
    h_rg = jnp.concatenate(h_tiles, axis=0)
    rgc_s[...] = h_tiles[-1]
    gelu = 0.5 * yr * (1.0 + jnp.tanh(0.7978845608028654 * (yr + 0.044715 * (yr * yr * yr))))
    o_s[:, dh:dh + dr] = _rms(h_rg * gelu, p512_ref[_P_RNORM:_P_RNORM + 1, :]).astype(BF16)

    mix = _dot(o_s[...], w_out_ref[...])
    x1 = x + gt1 * _rms(mix, p1024_ref[_P_GPOST:_P_GPOST + 1, :])
    x1_ref[0] = x1
    h2 = _rms(x1, p1024_ref[_P_GFFN:_P_GFFN + 1, :] * (1.0 + sc2)) + sh2
    h2_ref[0] = h2

    h_hi, h_mid, _ = _split3(h2)
    w_hi, w_mid, _ = _split3(w_rt_ref[...])
    lg = _dot(h_hi, w_hi) + (_dot(h_mid, w_hi) + _dot(h_hi, w_mid)) + b_rt_ref[...]
    lane = lax.broadcasted_iota(jnp.int32, (ts, LANES), 1)
    lane_f = lane.astype(F32)
    neg = jnp.float32(-jnp.inf)
    big = jnp.float32(2 * LANES)
    is_g = jnp.logical_and(lane >= GROUP_LANE0, lane < GROUP_LANE0 + N_GROUPS)
    gl = jnp.where(is_g, lg, neg)
    g_max = jnp.max(gl, axis=1, keepdims=True)
    g_star = jnp.min(jnp.where(gl == g_max, lane_f, big), axis=1, keepdims=True) - float(GROUP_LANE0)
    gate_g = 1.0 / jnp.sum(jnp.exp(gl - g_max), axis=1, keepdims=True)
    grp_f = jnp.floor(lane_f * (1.0 / EXPERTS_PER_GROUP))
    el = jnp.where(jnp.logical_and(lane < N_EXPERTS, grp_f == g_star), lg, neg)
    m1 = jnp.max(el, axis=1, keepdims=True)
    i1 = jnp.min(jnp.where(el == m1, lane_f, big), axis=1, keepdims=True)
    el2 = jnp.where(lane_f == i1, neg, el)
    m2 = jnp.max(el2, axis=1, keepdims=True)
    i2 = jnp.min(jnp.where(el2 == m2, lane_f, big), axis=1, keepdims=True)
    e21 = jnp.exp(m2 - m1)
    w0 = gate_g / (1.0 + e21)
    w1 = gate_g * e21 / (1.0 + e21)

    hit0 = lane_f == i1
    hit1 = lane_f == i2
    oh = jnp.where(jnp.logical_or(hit0, hit1), 1.0, 0.0)
    before = _dot(ltri_ref[...], oh.astype(BF16)) + cntc_s[0:1, :]
    rank0 = jnp.sum(jnp.where(hit0, before, 0.0), axis=1, keepdims=True)
    rank1 = jnp.sum(jnp.where(hit1, before, 0.0), axis=1, keepdims=True)
    cntc_s[...] = cntc_s[...] + jnp.sum(oh, axis=0, keepdims=True)
    cnt_ref[...] = cntc_s[...]

    route = jnp.where(lane == _R_E0, i1, 0.0)
    route = jnp.where(lane == _R_E1, i2, route)
    route = jnp.where(lane == _R_W0, w0, route)
    route = jnp.where(lane == _R_W1, w1, route)
    route = jnp.where(lane == _R_RANK0, rank0, route)
    route = jnp.where(lane == _R_RANK1, rank1, route)
    route_ref[...] = route


def _mix(x, mod, p1024, p512, w_in, w_ri, w_out, w_rt, b_rt):
    bsz, seq, d = x.shape
    ts = TILE_S
    dh = p512.shape[1]
    dr = p512.shape[1]
    nh = dh // HEAD_DIM
    n_s = seq // ts
    hmat_np, hmask_np = _hier_constants(CHUNK)
    hmat = jnp.asarray(hmat_np, BF16)
    hmask = jnp.asarray(hmask_np, F32)
    tt = np.arange(ts)
    ltri = jnp.asarray((tt[None, :] < tt[:, None]).astype(np.float32), BF16)

    def const(shape):
        return pl.BlockSpec(shape, lambda b, s: (0,) * len(shape))

    return pl.pallas_call(
        _mix_kernel,
        grid=(bsz, n_s),
        in_specs=[
            pl.BlockSpec((1, ts, d), lambda b, s: (b, s, 0)),
            pl.BlockSpec((1, mod.shape[1], d), lambda b, s: (b, 0, 0)),
            const(p1024.shape), const(p512.shape), const(w_in.shape), const(w_ri.shape),
            const(w_out.shape), const(w_rt.shape), const(b_rt.shape),
            const(hmat.shape), const(hmask.shape), const(ltri.shape),
        ],
        out_specs=[
            pl.BlockSpec((1, ts, d), lambda b, s: (b, s, 0)),
            pl.BlockSpec((1, ts, d), lambda b, s: (b, s, 0)),
            pl.BlockSpec((ts, LANES), lambda b, s: (b * n_s + s, 0)),
            pl.BlockSpec((SUBLANES, LANES), lambda b, s: (0, 0)),
        ],
        out_shape=[
            jax.ShapeDtypeStruct((bsz, seq, d), F32),
            jax.ShapeDtypeStruct((bsz, seq, d), F32),
            jax.ShapeDtypeStruct((bsz * seq, LANES), F32),
            jax.ShapeDtypeStruct((SUBLANES, LANES), F32),
        ],
        scratch_shapes=[
            pltpu.VMEM((ts, w_in.shape[1]), F32),
            pltpu.VMEM((CHUNK, dh), F32),
            pltpu.VMEM((CHUNK, dh), F32),
            pltpu.VMEM((hmat.shape[0], dh), F32),
            pltpu.VMEM((CHUNK, dh), F32),
            pltpu.VMEM((ts, dh + dr), BF16),
            pltpu.VMEM((nh, HEAD_DIM, HEAD_DIM), F32),
            pltpu.VMEM((SUBLANES, dr), F32),
            pltpu.VMEM((ts + SUBLANES, dr), F32),
            pltpu.VMEM((SUBLANES, LANES), F32),
        ],
        compiler_params=pltpu.CompilerParams(
            dimension_semantics=("arbitrary", "arbitrary"), vmem_limit_bytes=VMEM_LIMIT_BYTES),
        name="mix_route",
    )(x, mod, p1024, p512, w_in, w_ri, w_out, w_rt, b_rt, hmat, hmask, ltri)


def _dispatch_kernel(dest_ref, h2_ref, xb_in_ref, xb_ref, sem):
    del xb_in_ref
    tt = h2_ref.shape[0]

    def row_copy(t, slot):
        return pltpu.make_async_copy(h2_ref.at[pl.ds(t, 1)], xb_ref.at[pl.ds(slot, 1)], sem)

    def issue(t, carry):
        row_copy(t, dest_ref[0, 0, 2 * t]).start()
        row_copy(t, dest_ref[0, 0, 2 * t + 1]).start()
        return carry

    lax.fori_loop(0, tt, issue, 0, unroll=8)

    def drain(t, carry):
        row_copy(0, 0).wait()
        row_copy(0, 0).wait()
        return carry

    lax.fori_loop(0, tt, drain, 0, unroll=8)


def _dispatch(dest, h2, xb0):
    n_tok, d = h2.shape
    tt = TILE_T
    return pl.pallas_call(
        _dispatch_kernel,
        grid=(n_tok // tt,),
        in_specs=[
            pl.BlockSpec((1, 1, TOP_K * tt), lambda i: (i, 0, 0), memory_space=pltpu.SMEM),
            pl.BlockSpec((tt, d), lambda i: (i, 0)),
            pl.BlockSpec(memory_space=pl.ANY),
        ],
        out_specs=pl.BlockSpec(memory_space=pl.ANY),
        out_shape=jax.ShapeDtypeStruct(xb0.shape, xb0.dtype),
        scratch_shapes=[pltpu.SemaphoreType.DMA(())],
        input_output_aliases={2: 0},
        compiler_params=pltpu.CompilerParams(dimension_semantics=("arbitrary",)),
        name="dispatch_rows",
    )(dest, h2, xb0)


def _ffn_kernel(blk_e_ref, nblk_ref, xb_ref, w1_ref, w3_ref, w2_ref, yb_ref, w1b, w3b, w2b):
    b = pl.program_id(0)
    prev = blk_e_ref[jnp.maximum(b - 1, 0)]
    fresh = jnp.logical_or(b == 0, blk_e_ref[b] != prev)
    live = b < nblk_ref[0]

    @pl.when(jnp.logical_and(fresh, live))
    def _():
        w1b[...] = w1_ref[0].astype(BF16)
        w3b[...] = w3_ref[0].astype(BF16)
        w2b[...] = w2_ref[0].astype(BF16)

    @pl.when(live)
    def _():
        xv = xb_ref[...].astype(BF16)
        a = _dot(xv, w1b[...])
        g = _dot(xv, w3b[...])
        hact = (a * _sigmoid(a) * g).astype(BF16)
        yb_ref[...] = _dot(hact, w2b[...])

    @pl.when(jnp.logical_not(live))
    def _():
        yb_ref[...] = jnp.zeros_like(yb_ref)


def _ffn(blk_e, nblk, xb, w1, w3, w2):
    n_slots, d = xb.shape
    n_blocks = n_slots // MOE_BLOCK
    de = w1.shape[2]

    def x_map(b, be, nb):
        return (jnp.minimum(b, nb[0] - 1), 0)

    def w_map(b, be, nb):
        return (be[b], 0, 0)

    grid_spec = pltpu.PrefetchScalarGridSpec(
        num_scalar_prefetch=2,
        grid=(n_blocks,),
        in_specs=[
            pl.BlockSpec((MOE_BLOCK, d), x_map),
            pl.BlockSpec((1, d, de), w_map),
            pl.BlockSpec((1, d, de), w_map),
            pl.BlockSpec((1, de, d), w_map),
        ],
        out_specs=pl.BlockSpec((MOE_BLOCK, d), lambda b, be, nb: (b, 0)),
        scratch_shapes=[pltpu.VMEM((d, de), BF16), pltpu.VMEM((d, de), BF16), pltpu.VMEM((de, d), BF16)],
    )
    return pl.pallas_call(
        _ffn_kernel,
        grid_spec=grid_spec,
        out_shape=jax.ShapeDtypeStruct((n_slots, d), F32),
        compiler_params=pltpu.CompilerParams(
            dimension_semantics=("arbitrary",), vmem_limit_bytes=VMEM_LIMIT_BYTES),
        name="expert_ffn",
    )(blk_e, nblk, xb, w1, w3, w2)


def _combine_kernel(dest_ref, yb_ref, route_ref, x1_ref, mod_ref, g_ref, out_ref, ybuf, sem):
    tt = x1_ref.shape[0]

    def row_copy(slot, k, t):
        return pltpu.make_async_copy(yb_ref.at[pl.ds(slot, 1)], ybuf.at[k, pl.ds(t, 1)], sem)

    def issue(t, carry):
        row_copy(dest_ref[0, 0, 2 * t], 0, t).start()
        row_copy(dest_ref[0, 0, 2 * t + 1], 1, t).start()
        return carry

    lax.fori_loop(0, tt, issue, 0, unroll=8)

    def drain(t, carry):
        row_copy(0, 0, 0).wait()
        row_copy(0, 0, 0).wait()
        return carry

    lax.fori_loop(0, tt, drain, 0, unroll=8)

    w0 = route_ref[:, _R_W0:_R_W0 + 1]
    w1 = route_ref[:, _R_W1:_R_W1 + 1]
    y = ybuf[0] * w0 + ybuf[1] * w1
    gt2 = mod_ref[0, 5:6, :]
    out_ref[...] = x1_ref[...] + gt2 * _rms(y, g_ref[...])


def _combine(dest, yb, route, x1, mod, g_post_ffn, seq):
    n_tok, d = x1.shape
    tt = TILE_T
    per_b = seq // tt
    return pl.pallas_call(
        _combine_kernel,
        grid=(n_tok // tt,),
        in_specs=[
            pl.BlockSpec((1, 1, TOP_K * tt), lambda i: (i, 0, 0), memory_space=pltpu.SMEM),
            pl.BlockSpec(memory_space=pl.ANY),
            pl.BlockSpec((tt, LANES), lambda i: (i, 0)),
            pl.BlockSpec((tt, d), lambda i: (i, 0)),
            pl.BlockSpec((1, mod.shape[1], d), lambda i: (i // per_b, 0, 0)),
            pl.BlockSpec((1, d), lambda i: (0, 0)),
        ],
        out_specs=pl.BlockSpec((tt, d), lambda i: (i, 0)),
        out_shape=jax.ShapeDtypeStruct((n_tok, d), F32),
        scratch_shapes=[pltpu.VMEM((TOP_K, tt, d), F32), pltpu.SemaphoreType.DMA(())],
        compiler_params=pltpu.CompilerParams(dimension_semantics=("arbitrary",)),
        name="combine_rows",
    )(dest, yb, route, x1, mod, g_post_ffn)


def _block_diag(w):
    nb, bi, bo = w.shape
    eye = jnp.eye(nb, dtype=w.dtype)
    return (w[:, :, None, :] * eye[:, None, :, None]).reshape(nb * bi, nb * bo)


def kernel(x, c, w_ada, b_ada, g_pre_mix, g_post_mix, g_pre_ffn, g_post_ffn, w_in, hgrn_gamma, hgrn_norm_g, conv_w, conv_b, rg_w_r, rg_b_r, rg_w_i, rg_b_i, rg_lambda, rg_norm_g, w_out, w_router_group, b_router_group, w_router_expert, b_router_expert, w1, w3, w2):
    bsz, seq, d = x.shape
    depth = w_ada.shape[0]
    assert depth == 1 and hgrn_gamma.shape[0] == 2
    dh = hgrn_norm_g.shape[1]
    dr = rg_norm_g.shape[1]
    assert dh == dr and dh % HEAD_DIM == 0 and seq % TILE_S == 0 and TILE_S % CHUNK == 0
    assert w_router_expert.shape[2] == N_EXPERTS and w_router_group.shape[2] == N_GROUPS
    n_tok = bsz * seq

    c8 = jnp.pad(c, ((0, SUBLANES - bsz), (0, 0)))
    p512 = jnp.concatenate([
        hgrn_gamma, hgrn_norm_g, conv_w[0], conv_b, rg_b_r, rg_b_i, rg_lambda, rg_norm_g,
        jnp.zeros((4, dh), F32)], axis=0)
    p1024 = jnp.concatenate([g_pre_mix, g_post_mix, g_pre_ffn, jnp.zeros((5, d), F32)], axis=0)
    w_ri = jnp.concatenate([_block_diag(rg_w_r[0]), _block_diag(rg_w_i[0])], axis=1).astype(BF16)
    w_rt = jnp.concatenate([w_router_expert[0], w_router_group[0],
                            jnp.zeros((d, LANES - N_EXPERTS - N_GROUPS), F32)], axis=1)
    b_rt = jnp.concatenate([b_router_expert[0], b_router_group[0],
                            jnp.zeros((LANES - N_EXPERTS - N_GROUPS,), F32)])[None, :]

    mod = _ada(c8, w_ada[0], b_ada[0][None, :])[:bsz].reshape(bsz, 6, d)

    x1, h2, route, cnt = _mix(x, mod, p1024, p512, w_in[0].astype(BF16), w_ri, w_out[0].astype(BF16), w_rt, b_rt)

    counts = cnt[0, :N_EXPERTS].astype(jnp.int32)
    pcounts = (counts + MOE_BLOCK - 1) // MOE_BLOCK * MOE_BLOCK
    pends = jnp.cumsum(pcounts)
    pstarts = pends - pcounts
    n_blocks = -(-(n_tok * TOP_K) // MOE_BLOCK) + N_EXPERTS
    blk_start = jnp.arange(n_blocks, dtype=jnp.int32) * MOE_BLOCK
    blk_e = jnp.minimum(jnp.sum((pends[None, :] <= blk_start[:, None]).astype(jnp.int32), axis=1), N_EXPERTS - 1)
    nblk = (pends[-1:] // MOE_BLOCK).astype(jnp.int32)
    eids = route[:, _R_E0:_R_E1 + 1].astype(jnp.int32)
    ranks = route[:, _R_RANK0:_R_RANK1 + 1].astype(jnp.int32)
    is_e = eids[:, :, None] == jnp.arange(N_EXPERTS, dtype=jnp.int32)
    dest = (jnp.sum(jnp.where(is_e, pstarts, 0), axis=2) + ranks).reshape(n_tok // TILE_T, 1, TOP_K * TILE_T)

    xb0 = jnp.zeros((n_blocks * MOE_BLOCK, d), F32)
    xb = _dispatch(dest, h2.reshape(n_tok, d), xb0)
    yb = _ffn(blk_e, nblk, xb, w1[0], w3[0], w2[0])
    out = _combine(dest, yb, route, x1.reshape(n_tok, d), mod, g_post_ffn, seq)
    return out.reshape(bsz, seq, d)
```

```python
import functools

import numpy as np
import jax
import jax.numpy as jnp
from jax import lax
from jax.experimental import pallas as pl
from jax.experimental.pallas import tpu as pltpu

F32 = jnp.float32
BF16 = jnp.bfloat16

LANES = 128
SUBLANES = 8
VMEM_LIMIT_BYTES = 56 * 1024 * 1024

EPS = 1e-6
HEAD_DIM = 128
CHUNK = 128
SUB = 8
RG_C = 8.0
LOG2E = 1.4426950408889634
TINY = 1e-30
RG_BLOCKS = 8
CONV_WIDTH = 4
N_GROUPS = 4
EXPERTS_PER_GROUP = 8
N_EXPERTS = N_GROUPS * EXPERTS_PER_GROUP
TOP_K = 2
MOE_BLOCK = 256
TILE_S = 256
TILE_D = 1024
TILE_C = 512
GROUP_LANE0 = N_EXPERTS


def _levels(chunk):
    out, b = [], chunk // 2
    while b >= SUB:
        out.append(b)
        b //= 2
    return out


def _hier_constants(chunk):
    t = np.arange(chunk)
    low = (t[None, :] <= t[:, None]).astype(np.float32)
    mats, masks = [low], []
    for b in _levels(chunk):
        ref = (t // (2 * b)) * 2 * b + b - 1
        mats.append(low - low[ref])
        same = (t[:, None] // (2 * b)) == (t[None, :] // (2 * b))
        upper = (t[:, None] % (2 * b)) >= b
        lower = (t[None, :] % (2 * b)) < b
        masks.append((same & upper & lower).astype(np.float32))
    return np.concatenate(mats, 0), np.stack(masks, 0)


def _rms(v, g):
    return v * lax.rsqrt(jnp.mean(v * v, axis=-1, keepdims=True) + EPS) * g


def _sigmoid(v):
    return 1.0 / (1.0 + jnp.exp(-v))


def _dot(a, b):
    return jnp.dot(a, b, preferred_element_type=F32)


def _dot_nt(a, b):
    return lax.dot_general(a, b, (((1,), (1,)), ((), ())), preferred_element_type=F32)


def _pack_rows(v):
    half = v.shape[1] // 2
    vb = v.astype(BF16).astype(F32)
    hi = lax.bitcast_convert_type(vb[:, :half], jnp.uint32)
    lo = lax.bitcast_convert_type(vb[:, half:], jnp.uint32)
    return hi | lax.shift_right_logical(lo, jnp.uint32(16))


def _unpack_rows(w):
    hi = lax.bitcast_convert_type(w & jnp.uint32(0xFFFF0000), F32)
    lo = lax.bitcast_convert_type(lax.shift_left(w, jnp.uint32(16)), F32)
    return jnp.concatenate([hi, lo], axis=1)


def _split3(v):
    hi = v.astype(BF16)
    r1 = v - hi.astype(F32)
    mid = r1.astype(BF16)
    lo = (r1 - mid.astype(F32)).astype(BF16)
    return hi, mid, lo


def _ada_kernel(c_ref, w_ref, b_ref, o_ref):
    c = c_ref[...]
    sc = c * _sigmoid(c)
    s_hi, s_mid, _ = _split3(sc)
    w = w_ref[...]
    w_hi, w_mid, _ = _split3(w)
    acc = _dot(s_hi, w_hi) + (_dot(s_mid, w_hi) + _dot(s_hi, w_mid))
    o_ref[...] = acc + b_ref[...]


def _ada(c8, w_ada, b_ada):
    d, n = w_ada.shape
    tn = 512
    return pl.pallas_call(
        _ada_kernel,
        grid=(n // tn,),
        in_specs=[pl.BlockSpec((SUBLANES, d), lambda j: (0, 0)),
                  pl.BlockSpec((d, tn), lambda j: (0, j)),
                  pl.BlockSpec((1, tn), lambda j: (0, j))],
        out_specs=pl.BlockSpec((SUBLANES, tn), lambda j: (0, j)),
        out_shape=jax.ShapeDtypeStruct((SUBLANES, n), F32),
        compiler_params=pltpu.CompilerParams(dimension_semantics=("arbitrary",)),
        name="ada_mod",
    )(c8, w_ada, b_ada)


_P_GAMMA0, _P_GAMMA1, _P_HNORM, _P_CW0, _P_CB, _P_BR, _P_BI, _P_LAM, _P_RNORM = 0, 1, 2, 3, 7, 8, 9, 10, 11
_P_GPRE, _P_GPOST, _P_GFFN = 0, 1, 2
_R_E0, _R_E1, _R_W0, _R_W1, _R_RANK0, _R_RANK1 = 0, 1, 2, 3, 4, 5


def _mix_kernel(x_ref, mod_ref, p1024_ref, p512_ref, w_in_ref, w_ri_ref, w_out_ref, w_rt_ref, b_rt_ref,
                hmat_ref, hmask_ref, ltri_ref,
                x1_ref, h2_ref, route_ref, cnt_ref,
                proj_s, qs_s, kk_s, ea_s, od_s, o_s, st_s, rgc_s, ext_s, cntc_s):
    ts = x_ref.shape[1]
    dh = qs_s.shape[1]
    dr = rgc_s.shape[1]
    nh = dh // HEAD_DIM
    n_lv = hmask_ref.shape[0]
    si = pl.program_id(1)

    @pl.when(si == 0)
    def _():
        st_s[...] = jnp.zeros_like(st_s)
        rgc_s[...] = jnp.zeros_like(rgc_s)
        ext_s[0:SUBLANES, :] = jnp.zeros((SUBLANES, dr), F32)

    @pl.when(jnp.logical_and(pl.program_id(0) == 0, si == 0))
    def _():
        cntc_s[...] = jnp.zeros_like(cntc_s)

    x = x_ref[0]
    sh1, sc1, gt1 = mod_ref[0, 0:1, :], mod_ref[0, 1:2, :], mod_ref[0, 2:3, :]
    sh2, sc2 = mod_ref[0, 3:4, :], mod_ref[0, 4:5, :]

    h = _rms(x, p1024_ref[_P_GPRE:_P_GPRE + 1, :] * (1.0 + sc1)) + sh1
    proj_s[...] = _dot(h.astype(BF16), w_in_ref[...])

    g0 = p512_ref[_P_GAMMA0:_P_GAMMA0 + 1, :]
    g1 = p512_ref[_P_GAMMA1:_P_GAMMA1 + 1, :]
    gm = jnp.maximum(g0, g1)
    e0 = jnp.exp(g0 - gm)
    lb = e0 / (e0 + jnp.exp(g1 - gm))
    hnorm = p512_ref[_P_HNORM:_P_HNORM + 1, :]

    for c in range(ts // CHUNK):
        r0 = c * CHUNK
        q = proj_s[r0:r0 + CHUNK, 0:dh]
        fz = proj_s[r0:r0 + CHUNK, dh:2 * dh]
        qs_s[...] = q * _sigmoid(q)
        ez = jnp.exp(-jnp.abs(fz))
        inv = 1.0 / (1.0 + ez)
        pos = fz >= 0.0
        sg = jnp.where(pos, inv, ez * inv)
        sn = jnp.where(pos, ez * inv, inv)
        f = lb + (1.0 - lb) * sg
        kk_s[...] = (1.0 - lb) * sn
        g = jnp.log(f) * LOG2E
        ghi, gmid, glo = _split3(g)
        ea3 = _dot(hmat_ref[...], jnp.concatenate([ghi, gmid, glo], axis=1))
        ea_s[...] = (ea3[:, 0:dh] + ea3[:, dh:2 * dh]) + ea3[:, 2 * dh:3 * dh]

        for hd in range(nh):
            c0 = hd * HEAD_DIM
            qh = qs_s[:, c0:c0 + HEAD_DIM]
            kh = kk_s[:, c0:c0 + HEAD_DIM]
            vh = proj_s[r0:r0 + CHUNK, 2 * dh + c0:2 * dh + c0 + HEAD_DIM]
            b = ea_s[0:CHUNK, c0:c0 + HEAD_DIM]
            s_acc = jnp.zeros((CHUNK, CHUNK), F32)
            for lv in range(n_lv):
                e_lv = ea_s[(lv + 1) * CHUNK:(lv + 2) * CHUNK, c0:c0 + HEAD_DIM]
                w_lv = jnp.exp2(jnp.minimum(e_lv, -e_lv))
                s_acc = s_acc + hmask_ref[lv] * _dot_nt((qh * w_lv).astype(BF16), (kh * w_lv).astype(BF16))
            o = _dot(s_acc.astype(BF16), vh.astype(BF16))
            st_t = st_s[hd]
            o = o + _dot_nt((qh * jnp.exp2(b)).astype(BF16), st_t.astype(BF16))
            b_last = ea_s[CHUNK - 1:CHUNK, c0:c0 + HEAD_DIM]
            kd = (kh * jnp.exp2(b_last - b)).astype(BF16)
            st_s[hd] = st_t * jnp.exp2(b_last) + _dot(vh.T.astype(BF16), kd)
            od_s[:, c0:c0 + HEAD_DIM] = o

        row_i = lax.broadcasted_iota(jnp.int32, (SUB, dh), 0)
        ones_w = jnp.ones((HEAD_DIM, HEAD_DIM), BF16)

        for r in range(CHUNK // SUB):
            r8 = r * SUB
            q8 = qs_s[pl.ds(r8, SUB), :]
            k8 = kk_s[pl.ds(r8, SUB), :]
            b8 = ea_s[pl.ds(r8, SUB), :]
            v8 = proj_s[pl.ds(r0 + r8, SUB), 2 * dh:3 * dh]
            tiles, vjs = [], []
            for j in range(SUB):
                if j == 0:
                    kj, bj, vj = k8, b8, v8
                else:
                    kj = pltpu.roll(k8, j, 0)
                    bj = pltpu.roll(b8, j, 0)
                    vj = jnp.where(row_i >= j, pltpu.roll(v8, j, 0), 0.0)
                a = q8 * kj * jnp.exp2(jnp.minimum(b8 - bj, 0.0))
                for hd in range(nh):
                    tiles.append(a[:, hd * HEAD_DIM:(hd + 1) * HEAD_DIM])
                vjs.append(vj)
            red = _dot(jnp.concatenate(tiles, axis=0).astype(BF16), ones_w)
            acc = od_s[pl.ds(r8, SUB), :]
            for j in range(SUB):
                rj = [red[(j * nh + hd) * SUB:(j * nh + hd + 1) * SUB, :] for hd in range(nh)]
                acc = acc + jnp.concatenate(rj, axis=1) * vjs[j]
            od_s[pl.ds(r8, SUB), :] = acc

        gz = proj_s[r0:r0 + CHUNK, 3 * dh:4 * dh]
        sgz = gz * _sigmoid(gz)
        outs = []
        for hd in range(nh):
            c0 = hd * HEAD_DIM
            outs.append(_rms(od_s[:, c0:c0 + HEAD_DIM], hnorm[:, c0:c0 + HEAD_DIM]))
        o_s[r0:r0 + CHUNK, 0:dh] = (jnp.concatenate(outs, axis=1) * sgz).astype(BF16)

    xr = proj_s[:, 4 * dh:4 * dh + dr]
    yr = proj_s[:, 4 * dh + dr:4 * dh + 2 * dr]
    ext_s[SUBLANES:, :] = xr
    xc = p512_ref[_P_CW0 + CONV_WIDTH - 1:_P_CW0 + CONV_WIDTH, :] * xr + p512_ref[_P_CB:_P_CB + 1, :]
    for k in range(1, CONV_WIDTH):
        wk = p512_ref[_P_CW0 + CONV_WIDTH - 1 - k:_P_CW0 + CONV_WIDTH - k, :]
        xc = xc + wk * ext_s[SUBLANES - k:SUBLANES - k + ts, :]
    ext_s[0:SUBLANES, :] = xr[ts - SUBLANES:, :]

    gates = _dot(xc.astype(BF16), w_ri_ref[...])
    r_g = _sigmoid(gates[:, 0:dr] + p512_ref[_P_BR:_P_BR + 1, :])
    i_g = _sigmoid(gates[:, dr:2 * dr] + p512_ref[_P_BI:_P_BI + 1, :])
    lam = p512_ref[_P_LAM:_P_LAM + 1, :]
    log_sig = jnp.minimum(lam, 0.0) - jnp.log(1.0 + jnp.exp(-jnp.abs(lam)))
    a_t = jnp.exp(RG_C * r_g * log_sig)
    y_t = 1.0 - a_t * a_t
    u_t = y_t * lax.rsqrt(jnp.maximum(y_t, TINY)) * (i_g * xc)
    n_tiles = ts // SUBLANES
    a3 = a_t.reshape(n_tiles, SUBLANES, dr)
    u3 = u_t.reshape(n_tiles, SUBLANES, dr)
    row3 = lax.broadcasted_iota(jnp.int32, (n_tiles, SUBLANES, dr), 1)
    d = 1
    while d < SUBLANES:
        keep = row3 >= d
        a_prev = jnp.where(keep, pltpu.roll(a3, d, 1), 1.0)
        u_prev = jnp.where(keep, pltpu.roll(u3, d, 1), 0.0)
        u3 = a3 * u_prev + u3
        a3 = a3 * a_prev
        d *= 2
    carry = rgc_s[SUBLANES - 1:SUBLANES, :]
    h_tiles = []
    for t in range(n_tiles):
        h_t = a3[t] * carry + u3[t]
        h_tiles.append(h_t)
        carry = h_t[SUBLANES - 1:SUBLANES, :]
    h_rg = jnp.concatenate(h_tiles, axis=0)
    rgc_s[...] = h_tiles[-1]
    gelu = 0.5 * yr * (1.0 + jnp.tanh(0.7978845608028654 * (yr + 0.044715 * (yr * yr * yr))))
    o_s[:, dh:dh + dr] = _rms(h_rg * gelu, p512_ref[_P_RNORM:_P_RNORM + 1, :]).astype(BF16)

    mix = _dot(o_s[...], w_out_ref[...])
    x1 = x + gt1 * _rms(mix, p1024_ref[_P_GPOST:_P_GPOST + 1, :])
    x1_ref[0] = x1
    h2 = _rms(x1, p1024_ref[_P_GFFN:_P_GFFN + 1, :] * (1.0 + sc2)) + sh2
    h2_ref[0] = _pack_rows(h2)

    h_hi, h_mid, _ = _split3(h2)
    w_hi, w_mid, _ = _split3(w_rt_ref[...])
    lg = _dot(h_hi, w_hi) + (_dot(h_mid, w_hi) + _dot(h_hi, w_mid)) + b_rt_ref[...]
    lane = lax.broadcasted_iota(jnp.int32, (ts, LANES), 1)
    lane_f = lane.astype(F32)
    neg = jnp.float32(-jnp.inf)
    big = jnp.float32(2 * LANES)
    is_g = jnp.logical_and(lane >= GROUP_LANE0, lane < GROUP_LANE0 + N_GROUPS)
    gl = jnp.where(is_g, lg, neg)
    g_max = jnp.max(gl, axis=1, keepdims=True)
    g_star = jnp.min(jnp.where(gl == g_max, lane_f, big), axis=1, keepdims=True) - float(GROUP_LANE0)
    gate_g = 1.0 / jnp.sum(jnp.exp(gl - g_max), axis=1, keepdims=True)
    grp_f = jnp.floor(lane_f * (1.0 / EXPERTS_PER_GROUP))
    el = jnp.where(jnp.logical_and(lane < N_EXPERTS, grp_f == g_star), lg, neg)
    m1 = jnp.max(el, axis=1, keepdims=True)
    i1 = jnp.min(jnp.where(el == m1, lane_f, big), axis=1, keepdims=True)
    el2 = jnp.where(lane_f == i1, neg, el)
    m2 = jnp.max(el2, axis=1, keepdims=True)
    i2 = jnp.min(jnp.where(el2 == m2, lane_f, big), axis=1, keepdims=True)
    e21 = jnp.exp(m2 - m1)
    w0 = gate_g / (1.0 + e21)
    w1 = gate_g * e21 / (1.0 + e21)

    hit0 = lane_f == i1
    hit1 = lane_f == i2
    oh = jnp.where(jnp.logical_or(hit0, hit1), 1.0, 0.0)
    before = _dot(ltri_ref[...], oh.astype(BF16)) + cntc_s[0:1, :]
    rank0 = jnp.sum(jnp.where(hit0, before, 0.0), axis=1, keepdims=True)
    rank1 = jnp.sum(jnp.where(hit1, before, 0.0), axis=1, keepdims=True)
    cntc_s[...] = cntc_s[...] + jnp.sum(oh, axis=0, keepdims=True)
    cnt_ref[...] = cntc_s[...]

    route = jnp.where(lane == _R_E0, i1, 0.0)
    route = jnp.where(lane == _R_E1, i2, route)
    route = jnp.where(lane == _R_W0, w0, route)
    route = jnp.where(lane == _R_W1, w1, route)
    route = jnp.where(lane == _R_RANK0, rank0, route)
    route = jnp.where(lane == _R_RANK1, rank1, route)
    route_ref[...] = route


def _mix(x, mod, p1024, p512, w_in, w_ri, w_out, w_rt, b_rt):
    bsz, seq, d = x.shape
    ts = TILE_S
    dh = p512.shape[1]
    dr = p512.shape[1]
    nh = dh // HEAD_DIM
    n_s = seq // ts
    hmat_np, hmask_np = _hier_constants(CHUNK)
    hmat = jnp.asarray(hmat_np, BF16)
    hmask = jnp.asarray(hmask_np, F32)
    tt = np.arange(ts)
    ltri = jnp.asarray((tt[None, :] < tt[:, None]).astype(np.float32), BF16)

    def const(shape):
        return pl.BlockSpec(shape, lambda b, s: (0,) * len(shape))

    return pl.pallas_call(
        _mix_kernel,
        grid=(bsz, n_s),
        in_specs=[
            pl.BlockSpec((1, ts, d), lambda b, s: (b, s, 0)),
            pl.BlockSpec((1, mod.shape[1], d), lambda b, s: (b, 0, 0)),
            const(p1024.shape), const(p512.shape), const(w_in.shape), const(w_ri.shape),
            const(w_out.shape), const(w_rt.shape), const(b_rt.shape),
            const(hmat.shape), const(hmask.shape), const(ltri.shape),
        ],
        out_specs=[
            pl.BlockSpec((1, ts, d), lambda b, s: (b, s, 0)),
            pl.BlockSpec((1, ts, d // 2), lambda b, s: (b, s, 0)),
            pl.BlockSpec((ts, LANES), lambda b, s: (b * n_s + s, 0)),
            pl.BlockSpec((SUBLANES, LANES), lambda b, s: (0, 0)),
        ],
        out_shape=[
            jax.ShapeDtypeStruct((bsz, seq, d), F32),
            jax.ShapeDtypeStruct((bsz, seq, d // 2), jnp.uint32),
            jax.ShapeDtypeStruct((bsz * seq, LANES), F32),
            jax.ShapeDtypeStruct((SUBLANES, LANES), F32),
        ],
        scratch_shapes=[
            pltpu.VMEM((ts, w_in.shape[1]), F32),
            pltpu.VMEM((CHUNK, dh), F32),
            pltpu.VMEM((CHUNK, dh), F32),
            pltpu.VMEM((hmat.shape[0], dh), F32),
            pltpu.VMEM((CHUNK, dh), F32),
            pltpu.VMEM((ts, dh + dr), BF16),
            pltpu.VMEM((nh, HEAD_DIM, HEAD_DIM), F32),
            pltpu.VMEM((SUBLANES, dr), F32),
            pltpu.VMEM((ts + SUBLANES, dr), F32),
            pltpu.VMEM((SUBLANES, LANES), F32),
        ],
        compiler_params=pltpu.CompilerParams(
            dimension_semantics=("arbitrary", "arbitrary"), vmem_limit_bytes=VMEM_LIMIT_BYTES),
        name="mix_route",
    )(x, mod, p1024, p512, w_in, w_ri, w_out, w_rt, b_rt, hmat, hmask, ltri)


def _dispatch_kernel(tail_ref, dest_ref, h2_ref, xb_ref, zero_s, sem, zsem):
    tt = h2_ref.shape[0]

    @pl.when(pl.program_id(0) == 0)
    def _():
        zero_s[...] = jnp.zeros_like(zero_s)

        def tail_copy(e):
            row0 = pl.multiple_of(jnp.maximum(tail_ref[e], 0), MOE_BLOCK)
            return pltpu.make_async_copy(zero_s, xb_ref.at[pl.ds(row0, MOE_BLOCK)], zsem)

        for e in range(tail_ref.shape[0]):
            @pl.when(tail_ref[e] >= 0)
            def _():
                tail_copy(e).start()
        for e in range(tail_ref.shape[0]):
            @pl.when(tail_ref[e] >= 0)
            def _():
                tail_copy(e).wait()

    def row_copy(t, slot):
        return pltpu.make_async_copy(h2_ref.at[pl.ds(t, 1)], xb_ref.at[pl.ds(slot, 1)], sem)

    def issue(t, carry):
        row_copy(t, dest_ref[0, 0, 2 * t]).start()
        row_copy(t, dest_ref[0, 0, 2 * t + 1]).start()
        return carry

    lax.fori_loop(0, tt, issue, 0, unroll=8)

    def drain(t, carry):
        row_copy(0, 0).wait()
        row_copy(0, 0).wait()
        return carry

    lax.fori_loop(0, tt, drain, 0, unroll=8)


def _dispatch(tails, dest, h2, n_slots):
    n_tok, dw = h2.shape
    tt = dest.shape[2] // TOP_K
    grid_spec = pltpu.PrefetchScalarGridSpec(
        num_scalar_prefetch=1,
        grid=(n_tok // tt,),
        in_specs=[
            pl.BlockSpec((1, 1, TOP_K * tt), lambda i, tl: (i, 0, 0), memory_space=pltpu.SMEM),
            pl.BlockSpec((tt, dw), lambda i, tl: (i, 0)),
        ],
        out_specs=pl.BlockSpec(memory_space=pl.ANY),
        scratch_shapes=[pltpu.VMEM((MOE_BLOCK, dw), h2.dtype),
                        pltpu.SemaphoreType.DMA(()), pltpu.SemaphoreType.DMA(())],
    )
    return pl.pallas_call(
        _dispatch_kernel,
        grid_spec=grid_spec,
        out_shape=jax.ShapeDtypeStruct((n_slots, dw), h2.dtype),
        compiler_params=pltpu.CompilerParams(dimension_semantics=("arbitrary",)),
        name="dispatch_rows",
    )(tails, dest, h2)


def _ffn_kernel(blk_e_ref, nblk_ref, xb_ref, w1_ref, w3_ref, w2_ref, yb_ref, w1b, w3b, w2b):
    b = pl.program_id(0)
    prev = blk_e_ref[jnp.maximum(b - 1, 0)]
    fresh = jnp.logical_or(b == 0, blk_e_ref[b] != prev)
    live = b < nblk_ref[0]

    @pl.when(jnp.logical_and(fresh, live))
    def _():
        w1b[...] = w1_ref[0].astype(BF16)
        w3b[...] = w3_ref[0].astype(BF16)
        w2b[...] = w2_ref[0].astype(BF16)

    @pl.when(live)
    def _():
        xv = _unpack_rows(xb_ref[...]).astype(BF16)
        a = _dot(xv, w1b[...])
        g = _dot(xv, w3b[...])
        hact = (a * _sigmoid(a) * g).astype(BF16)
        yb_ref[...] = _pack_rows(_dot(hact, w2b[...]))

    @pl.when(jnp.logical_not(live))
    def _():
        yb_ref[...] = jnp.zeros_like(yb_ref)


def _ffn(blk_e, nblk, xb, w1, w3, w2):
    n_slots, dw = xb.shape
    n_blocks = n_slots // MOE_BLOCK
    d, de = w1.shape[1], w1.shape[2]

    def x_map(b, be, nb):
        return (jnp.minimum(b, nb[0] - 1), 0)

    def w_map(b, be, nb):
        return (be[b], 0, 0)

    grid_spec = pltpu.PrefetchScalarGridSpec(
        num_scalar_prefetch=2,
        grid=(n_blocks,),
        in_specs=[
            pl.BlockSpec((MOE_BLOCK, dw), x_map),
            pl.BlockSpec((1, d, de), w_map),
            pl.BlockSpec((1, d, de), w_map),
            pl.BlockSpec((1, de, d), w_map),
        ],
        out_specs=pl.BlockSpec((MOE_BLOCK, dw), lambda b, be, nb: (b, 0)),
        scratch_shapes=[pltpu.VMEM((d, de), BF16), pltpu.VMEM((d, de), BF16), pltpu.VMEM((de, d), BF16)],
    )
    return pl.pallas_call(
        _ffn_kernel,
        grid_spec=grid_spec,
        out_shape=jax.ShapeDtypeStruct((n_slots, dw), xb.dtype),
        compiler_params=pltpu.CompilerParams(
            dimension_semantics=("arbitrary",), vmem_limit_bytes=VMEM_LIMIT_BYTES),
        name="expert_ffn",
    )(blk_e, nblk, xb, w1, w3, w2)


def _combine_kernel(dest_ref, dest_next_ref, yb_ref, route_ref, x1_ref, mod_ref, g_ref, out_ref, ybuf, sem):
    tt = x1_ref.shape[0]
    i = pl.program_id(0)
    cur = i % 2

    def row_copy(slot, buf, k, t):
        return pltpu.make_async_copy(yb_ref.at[pl.ds(slot, 1)], ybuf.at[buf, k, pl.ds(t, 1)], sem.at[buf])

    def issue(d_ref, buf):
        def body(t, carry):
            row_copy(d_ref[0, 0, 2 * t], buf, 0, t).start()
            row_copy(d_ref[0, 0, 2 * t + 1], buf, 1, t).start()
            return carry
        lax.fori_loop(0, tt, body, 0, unroll=8)

    @pl.when(i == 0)
    def _():
        issue(dest_ref, 0)

    @pl.when(i + 1 < pl.num_programs(0))
    def _():
        issue(dest_next_ref, 1 - cur)

    def drain(t, carry):
        row_copy(0, cur, 0, 0).wait()
        row_copy(0, cur, 0, 0).wait()
        return carry

    lax.fori_loop(0, tt, drain, 0, unroll=8)

    w0 = route_ref[:, _R_W0:_R_W0 + 1]
    w1 = route_ref[:, _R_W1:_R_W1 + 1]
    y = _unpack_rows(ybuf[cur, 0]) * w0 + _unpack_rows(ybuf[cur, 1]) * w1
    gt2 = mod_ref[0, 5:6, :]
    out_ref[...] = x1_ref[...] + gt2 * _rms(y, g_ref[...])


def _combine(dest, yb, route, x1, mod, g_post_ffn, seq):
    n_tok, d = x1.shape
    tt = dest.shape[2] // TOP_K
    n_steps = n_tok // tt
    per_b = seq // tt
    return pl.pallas_call(
        _combine_kernel,
        grid=(n_steps,),
        in_specs=[
            pl.BlockSpec((1, 1, TOP_K * tt), lambda i: (i, 0, 0), memory_space=pltpu.SMEM),
            pl.BlockSpec((1, 1, TOP_K * tt), lambda i: (jnp.minimum(i + 1, n_steps - 1), 0, 0),
                         memory_space=pltpu.SMEM),
            pl.BlockSpec(memory_space=pl.ANY),
            pl.BlockSpec((tt, LANES), lambda i: (i, 0)),
            pl.BlockSpec((tt, d), lambda i: (i, 0)),
            pl.BlockSpec((1, mod.shape[1], d), lambda i: (i // per_b, 0, 0)),
            pl.BlockSpec((1, d), lambda i: (0, 0)),
        ],
        out_specs=pl.BlockSpec((tt, d), lambda i: (i, 0)),
        out_shape=jax.ShapeDtypeStruct((n_tok, d), F32),
        scratch_shapes=[pltpu.VMEM((2, TOP_K, tt, yb.shape[1]), yb.dtype), pltpu.SemaphoreType.DMA((2,))],
        compiler_params=pltpu.CompilerParams(
            dimension_semantics=("arbitrary",), vmem_limit_bytes=VMEM_LIMIT_BYTES),
        name="combine_rows",
    )(dest, dest, yb, route, x1, mod, g_post_ffn)


def _block_diag(w):
    nb, bi, bo = w.shape
    eye = jnp.eye(nb, dtype=w.dtype)
    return (w[:, :, None, :] * eye[:, None, :, None]).reshape(nb * bi, nb * bo)


def kernel(x, c, w_ada, b_ada, g_pre_mix, g_post_mix, g_pre_ffn, g_post_ffn, w_in, hgrn_gamma, hgrn_norm_g, conv_w, conv_b, rg_w_r, rg_b_r, rg_w_i, rg_b_i, rg_lambda, rg_norm_g, w_out, w_router_group, b_router_group, w_router_expert, b_router_expert, w1, w3, w2):
    bsz, seq, d = x.shape
    depth = w_ada.shape[0]
    assert depth == 1 and hgrn_gamma.shape[0] == 2
    dh = hgrn_norm_g.shape[1]
    dr = rg_norm_g.shape[1]
    assert dh == dr and dh % HEAD_DIM == 0 and seq % TILE_S == 0 and TILE_S % CHUNK == 0
    assert w_router_expert.shape[2] == N_EXPERTS and w_router_group.shape[2] == N_GROUPS
    n_tok = bsz * seq

    c8 = jnp.pad(c, ((0, SUBLANES - bsz), (0, 0)))
    p512 = jnp.concatenate([
        hgrn_gamma, hgrn_norm_g, conv_w[0], conv_b, rg_b_r, rg_b_i, rg_lambda, rg_norm_g,
        jnp.zeros((4, dh), F32)], axis=0)
    p1024 = jnp.concatenate([g_pre_mix, g_post_mix, g_pre_ffn, jnp.zeros((5, d), F32)], axis=0)
    w_ri = jnp.concatenate([_block_diag(rg_w_r[0]), _block_diag(rg_w_i[0])], axis=1).astype(BF16)
    w_rt = jnp.concatenate([w_router_expert[0], w_router_group[0],
                            jnp.zeros((d, LANES - N_EXPERTS - N_GROUPS), F32)], axis=1)
    b_rt = jnp.concatenate([b_router_expert[0], b_router_group[0],
                            jnp.zeros((LANES - N_EXPERTS - N_GROUPS,), F32)])[None, :]

    mod = _ada(c8, w_ada[0], b_ada[0][None, :])[:bsz].reshape(bsz, 6, d)

    x1, h2, route, cnt = _mix(x, mod, p1024, p512, w_in[0].astype(BF16), w_ri, w_out[0].astype(BF16), w_rt, b_rt)

    counts = cnt[0, :N_EXPERTS].astype(jnp.int32)
    pcounts = (counts + MOE_BLOCK - 1) // MOE_BLOCK * MOE_BLOCK
    pends = jnp.cumsum(pcounts)
    pstarts = pends - pcounts
    n_blocks = -(-(n_tok * TOP_K) // MOE_BLOCK) + N_EXPERTS
    blk_start = jnp.arange(n_blocks, dtype=jnp.int32) * MOE_BLOCK
    blk_e = jnp.minimum(jnp.sum((pends[None, :] <= blk_start[:, None]).astype(jnp.int32), axis=1), N_EXPERTS - 1)
    nblk = (pends[-1:] // MOE_BLOCK).astype(jnp.int32)
    eids = route[:, _R_E0:_R_E1 + 1].astype(jnp.int32)
    ranks = route[:, _R_RANK0:_R_RANK1 + 1].astype(jnp.int32)
    is_e = eids[:, :, None] == jnp.arange(N_EXPERTS, dtype=jnp.int32)
    dest = jnp.sum(jnp.where(is_e, pstarts, 0), axis=2) + ranks
    spare = pends[-1] + jnp.arange(N_EXPERTS, dtype=jnp.int32) * MOE_BLOCK
    tails = jnp.concatenate([jnp.where(pcounts > 0, pends - MOE_BLOCK, -1),
                             jnp.where(spare < n_blocks * MOE_BLOCK, spare, -1)]).astype(jnp.int32)

    xb = _dispatch(tails, dest.reshape(n_tok // TILE_D, 1, TOP_K * TILE_D), h2.reshape(n_tok, d // 2),
                   n_blocks * MOE_BLOCK)
    yb = _ffn(blk_e, nblk, xb, w1[0], w3[0], w2[0])
    out = _combine(dest.reshape(n_tok // TILE_C, 1, TOP_K * TILE_C), yb, route, x1.reshape(n_tok, d), mod,
                   g_post_ffn, seq)
    return out.reshape(bsz, seq, d)
```

```python
import functools

import numpy as np
import jax
import jax.numpy as jnp
from jax import lax
from jax.experimental import pallas as pl
from jax.experimental.pallas import tpu as pltpu

F32 = jnp.float32
BF16 = jnp.bfloat16

LANES = 128
SUBLANES = 8
VMEM_LIMIT_BYTES = 56 * 1024 * 1024

EPS = 1e-6
HEAD_DIM = 128
CHUNK = 128
SUB = 8
RG_C = 8.0
LOG2E = 1.4426950408889634
TINY = 1e-30
RG_BLOCKS = 8
CONV_WIDTH = 4
N_GROUPS = 4
EXPERTS_PER_GROUP = 8
N_EXPERTS = N_GROUPS * EXPERTS_PER_GROUP
TOP_K = 2
MOE_BLOCK = 256
TILE_S = 512
TILE_D = 1024
TILE_C = 512
GROUP_LANE0 = N_EXPERTS


def _levels(chunk):
    out, b = [], chunk // 2
    while b >= SUB:
        out.append(b)
        b //= 2
    return out


def _hier_constants(chunk):
    t = np.arange(chunk)
    low = (t[None, :] <= t[:, None]).astype(np.float32)
    mats, masks = [low], []
    for b in _levels(chunk):
        ref = (t // (2 * b)) * 2 * b + b - 1
        mats.append(low - low[ref])
        same = (t[:, None] // (2 * b)) == (t[None, :] // (2 * b))
        upper = (t[:, None] % (2 * b)) >= b
        lower = (t[None, :] % (2 * b)) < b
        masks.append((same & upper & lower).astype(np.float32))
    direct = [((t[None, :] == t[:, None] - j) & ((t[:, None] % SUB) >= j)).astype(np.float32) for j in range(SUB)]
    return np.concatenate(mats, 0), np.stack(masks, 0), np.stack(direct, 0)


def _rms(v, g):
    return v * lax.rsqrt(jnp.mean(v * v, axis=-1, keepdims=True) + EPS) * g


def _sigmoid(v):
    return 1.0 / (1.0 + jnp.exp(-v))


def _dot(a, b):
    return jnp.dot(a, b, preferred_element_type=F32)


def _dot_nt(a, b):
    return lax.dot_general(a, b, (((1,), (1,)), ((), ())), preferred_element_type=F32)


def _pack_rows(v):
    half = v.shape[1] // 2
    vb = v.astype(BF16).astype(F32)
    hi = lax.bitcast_convert_type(vb[:, :half], jnp.uint32)
    lo = lax.bitcast_convert_type(vb[:, half:], jnp.uint32)
    return hi | lax.shift_right_logical(lo, jnp.uint32(16))


def _unpack_rows(w):
    hi = lax.bitcast_convert_type(w & jnp.uint32(0xFFFF0000), F32)
    lo = lax.bitcast_convert_type(lax.shift_left(w, jnp.uint32(16)), F32)
    return jnp.concatenate([hi, lo], axis=1)


def _split3(v):
    hi = v.astype(BF16)
    r1 = v - hi.astype(F32)
    mid = r1.astype(BF16)
    lo = (r1 - mid.astype(F32)).astype(BF16)
    return hi, mid, lo


def _ada_kernel(c_ref, w_ref, b_ref, o_ref):
    c = c_ref[...]
    sc = c * _sigmoid(c)
    s_hi, s_mid, _ = _split3(sc)
    w = w_ref[...]
    w_hi, w_mid, _ = _split3(w)
    acc = _dot(s_hi, w_hi) + (_dot(s_mid, w_hi) + _dot(s_hi, w_mid))
    o_ref[...] = acc + b_ref[...]


def _ada(c8, w_ada, b_ada):
    d, n = w_ada.shape
    tn = 512
    return pl.pallas_call(
        _ada_kernel,
        grid=(n // tn,),
        in_specs=[pl.BlockSpec((SUBLANES, d), lambda j: (0, 0)),
                  pl.BlockSpec((d, tn), lambda j: (0, j)),
                  pl.BlockSpec((1, tn), lambda j: (0, j))],
        out_specs=pl.BlockSpec((SUBLANES, tn), lambda j: (0, j)),
        out_shape=jax.ShapeDtypeStruct((SUBLANES, n), F32),
        compiler_params=pltpu.CompilerParams(dimension_semantics=("arbitrary",)),
        name="ada_mod",
    )(c8, w_ada, b_ada)


_P_GAMMA0, _P_GAMMA1, _P_HNORM, _P_CW0, _P_CB, _P_BR, _P_BI, _P_LAM, _P_RNORM = 0, 1, 2, 3, 7, 8, 9, 10, 11
_P_GPRE, _P_GPOST, _P_GFFN = 0, 1, 2
_R_E0, _R_E1, _R_W0, _R_W1, _R_RANK0, _R_RANK1 = 0, 1, 2, 3, 4, 5


def _mix_kernel(x_ref, mod_ref, p1024_ref, p512_ref, w_in_ref, w_ri_ref, w_out_ref, w_rt_ref, b_rt_ref,
                hmat_ref, hmask_ref, dmask_ref, ltri_ref,
                x1_ref, h2_ref, route_ref, cnt_ref,
                proj_s, qs_s, kk_s, f_s, sd_s, ea_s, od_s, o_s, st_s, rgc_s, ext_s, cntc_s):
    ts = x_ref.shape[1]
    dh = qs_s.shape[1]
    dr = rgc_s.shape[1]
    nh = dh // HEAD_DIM
    n_lv = hmask_ref.shape[0]
    si = pl.program_id(1)

    @pl.when(si == 0)
    def _():
        st_s[...] = jnp.zeros_like(st_s)
        rgc_s[...] = jnp.zeros_like(rgc_s)
        ext_s[0:SUBLANES, :] = jnp.zeros((SUBLANES, dr), F32)

    @pl.when(jnp.logical_and(pl.program_id(0) == 0, si == 0))
    def _():
        cntc_s[...] = jnp.zeros_like(cntc_s)

    x = x_ref[0]
    sh1, sc1, gt1 = mod_ref[0, 0:1, :], mod_ref[0, 1:2, :], mod_ref[0, 2:3, :]
    sh2, sc2 = mod_ref[0, 3:4, :], mod_ref[0, 4:5, :]

    h = _rms(x, p1024_ref[_P_GPRE:_P_GPRE + 1, :] * (1.0 + sc1)) + sh1
    proj_s[...] = _dot(h.astype(BF16), w_in_ref[...])

    g0 = p512_ref[_P_GAMMA0:_P_GAMMA0 + 1, :]
    g1 = p512_ref[_P_GAMMA1:_P_GAMMA1 + 1, :]
    gm = jnp.maximum(g0, g1)
    e0 = jnp.exp(g0 - gm)
    lb = e0 / (e0 + jnp.exp(g1 - gm))
    hnorm = p512_ref[_P_HNORM:_P_HNORM + 1, :]

    for c in range(ts // CHUNK):
        r0 = c * CHUNK
        q = proj_s[r0:r0 + CHUNK, 0:dh]
        fz = proj_s[r0:r0 + CHUNK, dh:2 * dh]
        qs_s[...] = q * _sigmoid(q)
        ez = jnp.exp(-jnp.abs(fz))
        inv = 1.0 / (1.0 + ez)
        pos = fz >= 0.0
        sg = jnp.where(pos, inv, ez * inv)
        sn = jnp.where(pos, ez * inv, inv)
        f = lb + (1.0 - lb) * sg
        kk_s[...] = (1.0 - lb) * sn
        f_s[...] = f
        g = jnp.log(f) * LOG2E
        ghi = g.astype(BF16)
        glo = (g - ghi.astype(F32)).astype(BF16)
        ea2 = _dot(hmat_ref[...], jnp.concatenate([ghi, glo], axis=1))
        ea_s[...] = ea2[:, 0:dh] + ea2[:, dh:2 * dh]

        for r in range(CHUNK // SUB):
            r8 = r * SUB
            q8 = qs_s[r8:r8 + SUB, :]
            k8 = kk_s[r8:r8 + SUB, :]
            f8 = f_s[r8:r8 + SUB, :]
            qd = q8
            tile = None
            for j in range(SUB):
                if j > 0:
                    qd = qd * (f8 if j == 1 else pltpu.roll(f8, j - 1, 0))
                a = qd * (k8 if j == 0 else pltpu.roll(k8, j, 0))
                m = dmask_ref[j, r8:r8 + SUB, :]
                parts = [jnp.sum(a[:, hd * HEAD_DIM:(hd + 1) * HEAD_DIM], axis=1, keepdims=True) * m
                         for hd in range(nh)]
                contrib = jnp.concatenate(parts, axis=1)
                tile = contrib if tile is None else tile + contrib
            sd_s[r8:r8 + SUB, :] = tile

        for hd in range(nh):
            c0 = hd * HEAD_DIM
            qh = qs_s[:, c0:c0 + HEAD_DIM]
            kh = kk_s[:, c0:c0 + HEAD_DIM]
            vh = proj_s[r0:r0 + CHUNK, 2 * dh + c0:2 * dh + c0 + HEAD_DIM]
            b = ea_s[0:CHUNK, c0:c0 + HEAD_DIM]
            s_acc = sd_s[:, c0:c0 + CHUNK]
            for lv in range(n_lv):
                e_lv = ea_s[(lv + 1) * CHUNK:(lv + 2) * CHUNK, c0:c0 + HEAD_DIM]
                w_lv = jnp.exp2(jnp.minimum(e_lv, -e_lv))
                s_acc = s_acc + hmask_ref[lv] * _dot_nt((qh * w_lv).astype(BF16), (kh * w_lv).astype(BF16))
            o = _dot(s_acc.astype(BF16), vh.astype(BF16))
            st_t = st_s[hd]
            o = o + _dot_nt((qh * jnp.exp2(b)).astype(BF16), st_t.astype(BF16))
            b_last = ea_s[CHUNK - 1:CHUNK, c0:c0 + HEAD_DIM]
            kd = (kh * jnp.exp2(b_last - b)).astype(BF16)
            st_s[hd] = st_t * jnp.exp2(b_last) + _dot(vh.T.astype(BF16), kd)
            od_s[:, c0:c0 + HEAD_DIM] = o

        gz = proj_s[r0:r0 + CHUNK, 3 * dh:4 * dh]
        sgz = gz * _sigmoid(gz)
        outs = []
        for hd in range(nh):
            c0 = hd * HEAD_DIM
            outs.append(_rms(od_s[:, c0:c0 + HEAD_DIM], hnorm[:, c0:c0 + HEAD_DIM]))
        o_s[r0:r0 + CHUNK, 0:dh] = (jnp.concatenate(outs, axis=1) * sgz).astype(BF16)

    xr = proj_s[:, 4 * dh:4 * dh + dr]
    yr = proj_s[:, 4 * dh + dr:4 * dh + 2 * dr]
    ext_s[SUBLANES:, :] = xr
    xc = p512_ref[_P_CW0 + CONV_WIDTH - 1:_P_CW0 + CONV_WIDTH, :] * xr + p512_ref[_P_CB:_P_CB + 1, :]
    for k in range(1, CONV_WIDTH):
        wk = p512_ref[_P_CW0 + CONV_WIDTH - 1 - k:_P_CW0 + CONV_WIDTH - k, :]
        xc = xc + wk * ext_s[SUBLANES - k:SUBLANES - k + ts, :]
    ext_s[0:SUBLANES, :] = xr[ts - SUBLANES:, :]

    gates = _dot(xc.astype(BF16), w_ri_ref[...])
    r_g = _sigmoid(gates[:, 0:dr] + p512_ref[_P_BR:_P_BR + 1, :])
    i_g = _sigmoid(gates[:, dr:2 * dr] + p512_ref[_P_BI:_P_BI + 1, :])
    lam = p512_ref[_P_LAM:_P_LAM + 1, :]
    log_sig = jnp.minimum(lam, 0.0) - jnp.log(1.0 + jnp.exp(-jnp.abs(lam)))
    a_t = jnp.exp(RG_C * r_g * log_sig)
    y_t = 1.0 - a_t * a_t
    u_t = y_t * lax.rsqrt(jnp.maximum(y_t, TINY)) * (i_g * xc)
    n_tiles = ts // SUBLANES
    a3 = a_t.reshape(n_tiles, SUBLANES, dr)
    u3 = u_t.reshape(n_tiles, SUBLANES, dr)
    row3 = lax.broadcasted_iota(jnp.int32, (n_tiles, SUBLANES, dr), 1)
    d = 1
    while d < SUBLANES:
        keep = row3 >= d
        a_prev = jnp.where(keep, pltpu.roll(a3, d, 1), 1.0)
        u_prev = jnp.where(keep, pltpu.roll(u3, d, 1), 0.0)
        u3 = a3 * u_prev + u3
        a3 = a3 * a_prev
        d *= 2
    carry = rgc_s[SUBLANES - 1:SUBLANES, :]
    h_tiles = []
    for t in range(n_tiles):
        h_t = a3[t] * carry + u3[t]
        h_tiles.append(h_t)
        carry = h_t[SUBLANES - 1:SUBLANES, :]
    h_rg = jnp.concatenate(h_tiles, axis=0)
    rgc_s[...] = h_tiles[-1]
    gelu = 0.5 * yr * (1.0 + jnp.tanh(0.7978845608028654 * (yr + 0.044715 * (yr * yr * yr))))
    o_s[:, dh:dh + dr] = _rms(h_rg * gelu, p512_ref[_P_RNORM:_P_RNORM + 1, :]).astype(BF16)

    mix = _dot(o_s[...], w_out_ref[...])
    x1 = x + gt1 * _rms(mix, p1024_ref[_P_GPOST:_P_GPOST + 1, :])
    x1_ref[0] = x1
    h2 = _rms(x1, p1024_ref[_P_GFFN:_P_GFFN + 1, :] * (1.0 + sc2)) + sh2
    h2_ref[0] = _pack_rows(h2)

    h_hi, h_mid, _ = _split3(h2)
    w_hi, w_mid, _ = _split3(w_rt_ref[...])
    lg = _dot(h_hi, w_hi) + (_dot(h_mid, w_hi) + _dot(h_hi, w_mid)) + b_rt_ref[...]
    lane = lax.broadcasted_iota(jnp.int32, (ts, LANES), 1)
    lane_f = lane.astype(F32)
    neg = jnp.float32(-jnp.inf)
    big = jnp.float32(2 * LANES)
    is_g = jnp.logical_and(lane >= GROUP_LANE0, lane < GROUP_LANE0 + N_GROUPS)
    gl = jnp.where(is_g, lg, neg)
    g_max = jnp.max(gl, axis=1, keepdims=True)
    g_star = jnp.min(jnp.where(gl == g_max, lane_f, big), axis=1, keepdims=True) - float(GROUP_LANE0)
    gate_g = 1.0 / jnp.sum(jnp.exp(gl - g_max), axis=1, keepdims=True)
    grp_f = jnp.floor(lane_f * (1.0 / EXPERTS_PER_GROUP))
    el = jnp.where(jnp.logical_and(lane < N_EXPERTS, grp_f == g_star), lg, neg)
    m1 = jnp.max(el, axis=1, keepdims=True)
    i1 = jnp.min(jnp.where(el == m1, lane_f, big), axis=1, keepdims=True)
    el2 = jnp.where(lane_f == i1, neg, el)
    m2 = jnp.max(el2, axis=1, keepdims=True)
    i2 = jnp.min(jnp.where(el2 == m2, lane_f, big), axis=1, keepdims=True)
    e21 = jnp.exp(m2 - m1)
    w0 = gate_g / (1.0 + e21)
    w1 = gate_g * e21 / (1.0 + e21)

    hit0 = lane_f == i1
    hit1 = lane_f == i2
    oh = jnp.where(jnp.logical_or(hit0, hit1), 1.0, 0.0)
    before = _dot(ltri_ref[...], oh.astype(BF16)) + cntc_s[0:1, :]
    rank0 = jnp.sum(jnp.where(hit0, before, 0.0), axis=1, keepdims=True)
    rank1 = jnp.sum(jnp.where(hit1, before, 0.0), axis=1, keepdims=True)
    cntc_s[...] = cntc_s[...] + jnp.sum(oh, axis=0, keepdims=True)
    cnt_ref[...] = cntc_s[...]

    route = jnp.where(lane == _R_E0, i1, 0.0)
    route = jnp.where(lane == _R_E1, i2, route)
    route = jnp.where(lane == _R_W0, w0, route)
    route = jnp.where(lane == _R_W1, w1, route)
    route = jnp.where(lane == _R_RANK0, rank0, route)
    route = jnp.where(lane == _R_RANK1, rank1, route)
    route_ref[...] = route


def _mix(x, mod, p1024, p512, w_in, w_ri, w_out, w_rt, b_rt):
    bsz, seq, d = x.shape
    ts = TILE_S
    dh = p512.shape[1]
    dr = p512.shape[1]
    nh = dh // HEAD_DIM
    n_s = seq // ts
    assert CHUNK == HEAD_DIM
    hmat_np, hmask_np, dmask_np = _hier_constants(CHUNK)
    dmask = jnp.asarray(dmask_np, F32)
    hmat = jnp.asarray(hmat_np, BF16)
    hmask = jnp.asarray(hmask_np, F32)
    tt = np.arange(ts)
    ltri = jnp.asarray((tt[None, :] < tt[:, None]).astype(np.float32), BF16)

    def const(shape):
        return pl.BlockSpec(shape, lambda b, s: (0,) * len(shape))

    return pl.pallas_call(
        _mix_kernel,
        grid=(bsz, n_s),
        in_specs=[
            pl.BlockSpec((1, ts, d), lambda b, s: (b, s, 0)),
            pl.BlockSpec((1, mod.shape[1], d), lambda b, s: (b, 0, 0)),
            const(p1024.shape), const(p512.shape), const(w_in.shape), const(w_ri.shape),
            const(w_out.shape), const(w_rt.shape), const(b_rt.shape),
            const(hmat.shape), const(hmask.shape), const(dmask.shape), const(ltri.shape),
        ],
        out_specs=[
            pl.BlockSpec((1, ts, d), lambda b, s: (b, s, 0)),
            pl.BlockSpec((1, ts, d // 2), lambda b, s: (b, s, 0)),
            pl.BlockSpec((ts, LANES), lambda b, s: (b * n_s + s, 0)),
            pl.BlockSpec((SUBLANES, LANES), lambda b, s: (0, 0)),
        ],
        out_shape=[
            jax.ShapeDtypeStruct((bsz, seq, d), F32),
            jax.ShapeDtypeStruct((bsz, seq, d // 2), jnp.uint32),
            jax.ShapeDtypeStruct((bsz * seq, LANES), F32),
            jax.ShapeDtypeStruct((SUBLANES, LANES), F32),
        ],
        scratch_shapes=[
            pltpu.VMEM((ts, w_in.shape[1]), F32),
            pltpu.VMEM((CHUNK, dh), F32),
            pltpu.VMEM((CHUNK, dh), F32),
            pltpu.VMEM((CHUNK, dh), F32),
            pltpu.VMEM((CHUNK, dh), F32),
            pltpu.VMEM((hmat.shape[0], dh), F32),
            pltpu.VMEM((CHUNK, dh), F32),
            pltpu.VMEM((ts, dh + dr), BF16),
            pltpu.VMEM((nh, HEAD_DIM, HEAD_DIM), F32),
            pltpu.VMEM((SUBLANES, dr), F32),
            pltpu.VMEM((ts + SUBLANES, dr), F32),
            pltpu.VMEM((SUBLANES, LANES), F32),
        ],
        compiler_params=pltpu.CompilerParams(
            dimension_semantics=("arbitrary", "arbitrary"), vmem_limit_bytes=VMEM_LIMIT_BYTES),
        name="mix_route",
    )(x, mod, p1024, p512, w_in, w_ri, w_out, w_rt, b_rt, hmat, hmask, dmask, ltri)


def _dispatch_kernel(tail_ref, dest_ref, h2_ref, xb_ref, zero_s, sem, zsem):
    tt = h2_ref.shape[0]

    @pl.when(pl.program_id(0) == 0)
    def _():
        zero_s[...] = jnp.zeros_like(zero_s)

        def tail_copy(e):
            row0 = pl.multiple_of(jnp.maximum(tail_ref[e], 0), MOE_BLOCK)
            return pltpu.make_async_copy(zero_s, xb_ref.at[pl.ds(row0, MOE_BLOCK)], zsem)

        for e in range(tail_ref.shape[0]):
            @pl.when(tail_ref[e] >= 0)
            def _():
                tail_copy(e).start()
        for e in range(tail_ref.shape[0]):
            @pl.when(tail_ref[e] >= 0)
            def _():
                tail_copy(e).wait()

    def row_copy(t, slot):
        return pltpu.make_async_copy(h2_ref.at[pl.ds(t, 1)], xb_ref.at[pl.ds(slot, 1)], sem)

    def issue(t, carry):
        row_copy(t, dest_ref[0, 0, 2 * t]).start()
        row_copy(t, dest_ref[0, 0, 2 * t + 1]).start()
        return carry

    lax.fori_loop(0, tt, issue, 0, unroll=8)

    def drain(t, carry):
        row_copy(0, 0).wait()
        row_copy(0, 0).wait()
        return carry

    lax.fori_loop(0, tt, drain, 0, unroll=8)


def _dispatch(tails, dest, h2, n_slots):
    n_tok, dw = h2.shape
    tt = dest.shape[2] // TOP_K
    grid_spec = pltpu.PrefetchScalarGridSpec(
        num_scalar_prefetch=1,
        grid=(n_tok // tt,),
        in_specs=[
            pl.BlockSpec((1, 1, TOP_K * tt), lambda i, tl: (i, 0, 0), memory_space=pltpu.SMEM),
            pl.BlockSpec((tt, dw), lambda i, tl: (i, 0)),
        ],
        out_specs=pl.BlockSpec(memory_space=pl.ANY),
        scratch_shapes=[pltpu.VMEM((MOE_BLOCK, dw), h2.dtype),
                        pltpu.SemaphoreType.DMA(()), pltpu.SemaphoreType.DMA(())],
    )
    return pl.pallas_call(
        _dispatch_kernel,
        grid_spec=grid_spec,
        out_shape=jax.ShapeDtypeStruct((n_slots, dw), h2.dtype),
        compiler_params=pltpu.CompilerParams(dimension_semantics=("arbitrary",)),
        name="dispatch_rows",
    )(tails, dest, h2)


def _ffn_kernel(blk_e_ref, nblk_ref, xb_ref, w1_ref, w3_ref, w2_ref, yb_ref, w1b, w3b, w2b):
    b = pl.program_id(0)
    prev = blk_e_ref[jnp.maximum(b - 1, 0)]
    fresh = jnp.logical_or(b == 0, blk_e_ref[b] != prev)
    live = b < nblk_ref[0]

    @pl.when(jnp.logical_and(fresh, live))
    def _():
        w1b[...] = w1_ref[0].astype(BF16)
        w3b[...] = w3_ref[0].astype(BF16)
        w2b[...] = w2_ref[0].astype(BF16)

    @pl.when(live)
    def _():
        xv = _unpack_rows(xb_ref[...]).astype(BF16)
        a = _dot(xv, w1b[...])
        g = _dot(xv, w3b[...])
        hact = (a * _sigmoid(a) * g).astype(BF16)
        yb_ref[...] = _pack_rows(_dot(hact, w2b[...]))

    @pl.when(jnp.logical_not(live))
    def _():
        yb_ref[...] = jnp.zeros_like(yb_ref)


def _ffn(blk_e, nblk, xb, w1, w3, w2):
    n_slots, dw = xb.shape
    n_blocks = n_slots // MOE_BLOCK
    d, de = w1.shape[1], w1.shape[2]

    def x_map(b, be, nb):
        return (jnp.minimum(b, nb[0] - 1), 0)

    def w_map(b, be, nb):
        return (be[b], 0, 0)

    grid_spec = pltpu.PrefetchScalarGridSpec(
        num_scalar_prefetch=2,
        grid=(n_blocks,),
        in_specs=[
            pl.BlockSpec((MOE_BLOCK, dw), x_map),
            pl.BlockSpec((1, d, de), w_map),
            pl.BlockSpec((1, d, de), w_map),
            pl.BlockSpec((1, de, d), w_map),
        ],
        out_specs=pl.BlockSpec((MOE_BLOCK, dw), lambda b, be, nb: (b, 0)),
        scratch_shapes=[pltpu.VMEM((d, de), BF16), pltpu.VMEM((d, de), BF16), pltpu.VMEM((de, d), BF16)],
    )
    return pl.pallas_call(
        _ffn_kernel,
        grid_spec=grid_spec,
        out_shape=jax.ShapeDtypeStruct((n_slots, dw), xb.dtype),
        compiler_params=pltpu.CompilerParams(
            dimension_semantics=("arbitrary",), vmem_limit_bytes=VMEM_LIMIT_BYTES),
        name="expert_ffn",
    )(blk_e, nblk, xb, w1, w3, w2)


def _combine_kernel(dest_ref, dest_next_ref, yb_ref, route_ref, x1_ref, mod_ref, g_ref, out_ref, ybuf, sem):
    tt = x1_ref.shape[0]
    i = pl.program_id(0)
    cur = i % 2

    def row_copy(slot, buf, k, t):
        return pltpu.make_async_copy(yb_ref.at[pl.ds(slot, 1)], ybuf.at[buf, k, pl.ds(t, 1)], sem.at[buf])

    def issue(d_ref, buf):
        def body(t, carry):
            row_copy(d_ref[0, 0, 2 * t], buf, 0, t).start()
            row_copy(d_ref[0, 0, 2 * t + 1], buf, 1, t).start()
            return carry
        lax.fori_loop(0, tt, body, 0, unroll=8)

    @pl.when(i == 0)
    def _():
        issue(dest_ref, 0)

    @pl.when(i + 1 < pl.num_programs(0))
    def _():
        issue(dest_next_ref, 1 - cur)

    def drain(t, carry):
        row_copy(0, cur, 0, 0).wait()
        row_copy(0, cur, 0, 0).wait()
        return carry

    lax.fori_loop(0, tt, drain, 0, unroll=8)

    w0 = route_ref[:, _R_W0:_R_W0 + 1]
    w1 = route_ref[:, _R_W1:_R_W1 + 1]
    y = _unpack_rows(ybuf[cur, 0]) * w0 + _unpack_rows(ybuf[cur, 1]) * w1
    gt2 = mod_ref[0, 5:6, :]
    out_ref[...] = x1_ref[...] + gt2 * _rms(y, g_ref[...])


def _combine(dest, yb, route, x1, mod, g_post_ffn, seq):
    n_tok, d = x1.shape
    tt = dest.shape[2] // TOP_K
    n_steps = n_tok // tt
    per_b = seq // tt
    return pl.pallas_call(
        _combine_kernel,
        grid=(n_steps,),
        in_specs=[
            pl.BlockSpec((1, 1, TOP_K * tt), lambda i: (i, 0, 0), memory_space=pltpu.SMEM),
            pl.BlockSpec((1, 1, TOP_K * tt), lambda i: (jnp.minimum(i + 1, n_steps - 1), 0, 0),
                         memory_space=pltpu.SMEM),
            pl.BlockSpec(memory_space=pl.ANY),
            pl.BlockSpec((tt, LANES), lambda i: (i, 0)),
            pl.BlockSpec((tt, d), lambda i: (i, 0)),
            pl.BlockSpec((1, mod.shape[1], d), lambda i: (i // per_b, 0, 0)),
            pl.BlockSpec((1, d), lambda i: (0, 0)),
        ],
        out_specs=pl.BlockSpec((tt, d), lambda i: (i, 0)),
        out_shape=jax.ShapeDtypeStruct((n_tok, d), F32),
        scratch_shapes=[pltpu.VMEM((2, TOP_K, tt, yb.shape[1]), yb.dtype), pltpu.SemaphoreType.DMA((2,))],
        compiler_params=pltpu.CompilerParams(
            dimension_semantics=("arbitrary",), vmem_limit_bytes=VMEM_LIMIT_BYTES),
        name="combine_rows",
    )(dest, dest, yb, route, x1, mod, g_post_ffn)


def _block_diag(w):
    nb, bi, bo = w.shape
    eye = jnp.eye(nb, dtype=w.dtype)
    return (w[:, :, None, :] * eye[:, None, :, None]).reshape(nb * bi, nb * bo)


def kernel(x, c, w_ada, b_ada, g_pre_mix, g_post_mix, g_pre_ffn, g_post_ffn, w_in, hgrn_gamma, hgrn_norm_g, conv_w, conv_b, rg_w_r, rg_b_r, rg_w_i, rg_b_i, rg_lambda, rg_norm_g, w_out, w_router_group, b_router_group, w_router_expert, b_router_expert, w1, w3, w2):
    bsz, seq, d = x.shape
    depth = w_ada.shape[0]
    assert depth == 1 and hgrn_gamma.shape[0] == 2
    dh = hgrn_norm_g.shape[1]
    dr = rg_norm_g.shape[1]
    assert dh == dr and dh % HEAD_DIM == 0 and seq % TILE_S == 0 and TILE_S % CHUNK == 0
    assert w_router_expert.shape[2] == N_EXPERTS and w_router_group.shape[2] == N_GROUPS
    n_tok = bsz * seq

    c8 = jnp.pad(c, ((0, SUBLANES - bsz), (0, 0)))
    p512 = jnp.concatenate([
        hgrn_gamma, hgrn_norm_g, conv_w[0], conv_b, rg_b_r, rg_b_i, rg_lambda, rg_norm_g,
        jnp.zeros((4, dh), F32)], axis=0)
    p1024 = jnp.concatenate([g_pre_mix, g_post_mix, g_pre_ffn, jnp.zeros((5, d), F32)], axis=0)
    w_ri = jnp.concatenate([_block_diag(rg_w_r[0]), _block_diag(rg_w_i[0])], axis=1).astype(BF16)
    w_rt = jnp.concatenate([w_router_expert[0], w_router_group[0],
                            jnp.zeros((d, LANES - N_EXPERTS - N_GROUPS), F32)], axis=1)
    b_rt = jnp.concatenate([b_router_expert[0], b_router_group[0],
                            jnp.zeros((LANES - N_EXPERTS - N_GROUPS,), F32)])[None, :]

    mod = _ada(c8, w_ada[0], b_ada[0][None, :])[:bsz].reshape(bsz, 6, d)

    x1, h2, route, cnt = _mix(x, mod, p1024, p512, w_in[0].astype(BF16), w_ri, w_out[0].astype(BF16), w_rt, b_rt)

    counts = cnt[0, :N_EXPERTS].astype(jnp.int32)
    pcounts = (counts + MOE_BLOCK - 1) // MOE_BLOCK * MOE_BLOCK
    pends = jnp.cumsum(pcounts)
    pstarts = pends - pcounts
    n_blocks = -(-(n_tok * TOP_K) // MOE_BLOCK) + N_EXPERTS
    blk_start = jnp.arange(n_blocks, dtype=jnp.int32) * MOE_BLOCK
    blk_e = jnp.minimum(jnp.sum((pends[None, :] <= blk_start[:, None]).astype(jnp.int32), axis=1), N_EXPERTS - 1)
    nblk = (pends[-1:] // MOE_BLOCK).astype(jnp.int32)
    eids = route[:, _R_E0:_R_E1 + 1].astype(jnp.int32)
    ranks = route[:, _R_RANK0:_R_RANK1 + 1].astype(jnp.int32)
    is_e = eids[:, :, None] == jnp.arange(N_EXPERTS, dtype=jnp.int32)
    dest = jnp.sum(jnp.where(is_e, pstarts, 0), axis=2) + ranks
    spare = pends[-1] + jnp.arange(N_EXPERTS, dtype=jnp.int32) * MOE_BLOCK
    tails = jnp.concatenate([jnp.where(pcounts > 0, pends - MOE_BLOCK, -1),
                             jnp.where(spare < n_blocks * MOE_BLOCK, spare, -1)]).astype(jnp.int32)

    xb = _dispatch(tails, dest.reshape(n_tok // TILE_D, 1, TOP_K * TILE_D), h2.reshape(n_tok, d // 2),
                   n_blocks * MOE_BLOCK)
    yb = _ffn(blk_e, nblk, xb, w1[0], w3[0], w2[0])
    out = _combine(dest.reshape(n_tok // TILE_C, 1, TOP_K * TILE_C), yb, route, x1.reshape(n_tok, d), mod,
                   g_post_ffn, seq)
    return out.reshape(bsz, seq, d)
```

```python
import functools

import numpy as np
import jax
import jax.numpy as jnp
from jax import lax
from jax.experimental import pallas as pl
from jax.experimental.pallas import tpu as pltpu

F32 = jnp.float32
BF16 = jnp.bfloat16

LANES = 128
SUBLANES = 8
ROW_TILES = 4
VMEM_LIMIT_BYTES = 56 * 1024 * 1024

EPS = 1e-6
HEAD_DIM = 128
CHUNK = 128
SUB = 8
RG_C = 8.0
LOG2E = 1.4426950408889634
TINY = 1e-30
RG_BLOCKS = 8
CONV_WIDTH = 4
N_GROUPS = 4
EXPERTS_PER_GROUP = 8
N_EXPERTS = N_GROUPS * EXPERTS_PER_GROUP
TOP_K = 2
MOE_BLOCK = 256
TILE_S = 512
TILE_D = 1024
TILE_C = 512
GROUP_LANE0 = N_EXPERTS


def _levels(chunk):
    out, b = [], chunk // 2
    while b >= SUB:
        out.append(b)
        b //= 2
    return out


def _hier_constants(chunk):
    t = np.arange(chunk)
    low = (t[None, :] <= t[:, None]).astype(np.float32)
    mats, masks = [low], []
    for b in _levels(chunk):
        ref = (t // (2 * b)) * 2 * b + b - 1
        mats.append(low - low[ref])
        same = (t[:, None] // (2 * b)) == (t[None, :] // (2 * b))
        upper = (t[:, None] % (2 * b)) >= b
        lower = (t[None, :] % (2 * b)) < b
        masks.append((same & upper & lower).astype(np.float32))
    direct = [((t[None, :] == t[:, None] - j) & ((t[:, None] % SUB) >= j)).astype(np.float32) for j in range(SUB)]
    return np.concatenate(mats, 0), np.stack(masks, 0), np.stack(direct, 0)


def _rms(v, g):
    return v * lax.rsqrt(jnp.mean(v * v, axis=-1, keepdims=True) + EPS) * g


def _sigmoid(v):
    return 1.0 / (1.0 + jnp.exp(-v))


def _dot(a, b):
    return jnp.dot(a, b, preferred_element_type=F32)


def _dot_nt(a, b):
    return lax.dot_general(a, b, (((1,), (1,)), ((), ())), preferred_element_type=F32)


def _pack_rows(v):
    half = v.shape[1] // 2
    vb = v.astype(BF16).astype(F32)
    hi = lax.bitcast_convert_type(vb[:, :half], jnp.uint32)
    lo = lax.bitcast_convert_type(vb[:, half:], jnp.uint32)
    return hi | lax.shift_right_logical(lo, jnp.uint32(16))


def _unpack_rows(w):
    hi = lax.bitcast_convert_type(w & jnp.uint32(0xFFFF0000), F32)
    lo = lax.bitcast_convert_type(lax.shift_left(w, jnp.uint32(16)), F32)
    return jnp.concatenate([hi, lo], axis=1)


def _store_row_tiles(ref, w):
    n = w.shape[0]
    for j in range(ROW_TILES):
        ref[pl.ds(j, n, stride=ROW_TILES), :] = w[:, j * LANES:(j + 1) * LANES]


def _load_row_tiles(ref):
    n = ref.shape[0] // ROW_TILES
    return jnp.concatenate([ref[pl.ds(j, n, stride=ROW_TILES), :] for j in range(ROW_TILES)], axis=1)


def _split3(v):
    hi = v.astype(BF16)
    r1 = v - hi.astype(F32)
    mid = r1.astype(BF16)
    lo = (r1 - mid.astype(F32)).astype(BF16)
    return hi, mid, lo


def _ada_kernel(c_ref, w_ref, b_ref, o_ref):
    c = c_ref[...]
    sc = c * _sigmoid(c)
    s_hi, s_mid, _ = _split3(sc)
    w = w_ref[...]
    w_hi, w_mid, _ = _split3(w)
    acc = _dot(s_hi, w_hi) + (_dot(s_mid, w_hi) + _dot(s_hi, w_mid))
    o_ref[...] = acc + b_ref[...]


def _ada(c8, w_ada, b_ada):
    d, n = w_ada.shape
    tn = 512
    return pl.pallas_call(
        _ada_kernel,
        grid=(n // tn,),
        in_specs=[pl.BlockSpec((SUBLANES, d), lambda j: (0, 0)),
                  pl.BlockSpec((d, tn), lambda j: (0, j)),
                  pl.BlockSpec((1, tn), lambda j: (0, j))],
        out_specs=pl.BlockSpec((SUBLANES, tn), lambda j: (0, j)),
        out_shape=jax.ShapeDtypeStruct((SUBLANES, n), F32),
        compiler_params=pltpu.CompilerParams(dimension_semantics=("arbitrary",)),
        name="ada_mod",
    )(c8, w_ada, b_ada)


_P_GAMMA0, _P_GAMMA1, _P_HNORM, _P_CW0, _P_CB, _P_BR, _P_BI, _P_LAM, _P_RNORM = 0, 1, 2, 3, 7, 8, 9, 10, 11
_P_GPRE, _P_GPOST, _P_GFFN = 0, 1, 2
_R_E0, _R_E1, _R_W0, _R_W1, _R_RANK0, _R_RANK1 = 0, 1, 2, 3, 4, 5


def _mix_kernel(x_ref, mod_ref, p1024_ref, p512_ref, w_in_ref, w_ri_ref, w_out_ref, w_rt_ref, b_rt_ref,
                hmat_ref, hmask_ref, dmask_ref, ltri_ref,
                x1_ref, h2_ref, route_ref, cnt_ref,
                proj_s, qs_s, kk_s, f_s, sd_s, ea_s, od_s, o_s, st_s, rgc_s, ext_s, cntc_s):
    ts = x_ref.shape[1]
    dh = qs_s.shape[1]
    dr = rgc_s.shape[1]
    nh = dh // HEAD_DIM
    n_lv = hmask_ref.shape[0]
    si = pl.program_id(1)

    @pl.when(si == 0)
    def _():
        st_s[...] = jnp.zeros_like(st_s)
        rgc_s[...] = jnp.zeros_like(rgc_s)
        ext_s[0:SUBLANES, :] = jnp.zeros((SUBLANES, dr), F32)

    @pl.when(jnp.logical_and(pl.program_id(0) == 0, si == 0))
    def _():
        cntc_s[...] = jnp.zeros_like(cntc_s)

    x = x_ref[0]
    sh1, sc1, gt1 = mod_ref[0, 0:1, :], mod_ref[0, 1:2, :], mod_ref[0, 2:3, :]
    sh2, sc2 = mod_ref[0, 3:4, :], mod_ref[0, 4:5, :]

    h = _rms(x, p1024_ref[_P_GPRE:_P_GPRE + 1, :] * (1.0 + sc1)) + sh1
    proj_s[...] = _dot(h.astype(BF16), w_in_ref[...])

    g0 = p512_ref[_P_GAMMA0:_P_GAMMA0 + 1, :]
    g1 = p512_ref[_P_GAMMA1:_P_GAMMA1 + 1, :]
    gm = jnp.maximum(g0, g1)
    e0 = jnp.exp(g0 - gm)
    lb = e0 / (e0 + jnp.exp(g1 - gm))
    hnorm = p512_ref[_P_HNORM:_P_HNORM + 1, :]

    for c in range(ts // CHUNK):
        r0 = c * CHUNK
        q = proj_s[r0:r0 + CHUNK, 0:dh]
        fz = proj_s[r0:r0 + CHUNK, dh:2 * dh]
        qs_s[...] = q * _sigmoid(q)
        ez = jnp.exp(-jnp.abs(fz))
        inv = 1.0 / (1.0 + ez)
        pos = fz >= 0.0
        sg = jnp.where(pos, inv, ez * inv)
        sn = jnp.where(pos, ez * inv, inv)
        f = lb + (1.0 - lb) * sg
        kk_s[...] = (1.0 - lb) * sn
        f_s[...] = f
        g = jnp.log(f) * LOG2E
        ghi = g.astype(BF16)
        glo = (g - ghi.astype(F32)).astype(BF16)
        ea2 = _dot(hmat_ref[...], jnp.concatenate([ghi, glo], axis=1))
        ea_s[...] = ea2[:, 0:dh] + ea2[:, dh:2 * dh]

        for r in range(CHUNK // SUB):
            r8 = r * SUB
            q8 = qs_s[r8:r8 + SUB, :]
            k8 = kk_s[r8:r8 + SUB, :]
            f8 = f_s[r8:r8 + SUB, :]
            qd = q8
            tile = None
            for j in range(SUB):
                if j > 0:
                    qd = qd * (f8 if j == 1 else pltpu.roll(f8, j - 1, 0))
                a = qd * (k8 if j == 0 else pltpu.roll(k8, j, 0))
                m = dmask_ref[j, r8:r8 + SUB, :]
                parts = [jnp.sum(a[:, hd * HEAD_DIM:(hd + 1) * HEAD_DIM], axis=1, keepdims=True) * m
                         for hd in range(nh)]
                contrib = jnp.concatenate(parts, axis=1)
                tile = contrib if tile is None else tile + contrib
            sd_s[r8:r8 + SUB, :] = tile

        for hd in range(nh):
            c0 = hd * HEAD_DIM
            qh = qs_s[:, c0:c0 + HEAD_DIM]
            kh = kk_s[:, c0:c0 + HEAD_DIM]
            vh = proj_s[r0:r0 + CHUNK, 2 * dh + c0:2 * dh + c0 + HEAD_DIM]
            b = ea_s[0:CHUNK, c0:c0 + HEAD_DIM]
            s_acc = sd_s[:, c0:c0 + CHUNK]
            for lv in range(n_lv):
                e_lv = ea_s[(lv + 1) * CHUNK:(lv + 2) * CHUNK, c0:c0 + HEAD_DIM]
                w_lv = jnp.exp2(jnp.minimum(e_lv, -e_lv))
                s_acc = s_acc + hmask_ref[lv] * _dot_nt((qh * w_lv).astype(BF16), (kh * w_lv).astype(BF16))
            o = _dot(s_acc.astype(BF16), vh.astype(BF16))
            st_t = st_s[hd]
            o = o + _dot_nt((qh * jnp.exp2(b)).astype(BF16), st_t.astype(BF16))
            b_last = ea_s[CHUNK - 1:CHUNK, c0:c0 + HEAD_DIM]
            kd = (kh * jnp.exp2(b_last - b)).astype(BF16)
            st_s[hd] = st_t * jnp.exp2(b_last) + _dot(vh.T.astype(BF16), kd)
            od_s[:, c0:c0 + HEAD_DIM] = o

        gz = proj_s[r0:r0 + CHUNK, 3 * dh:4 * dh]
        sgz = gz * _sigmoid(gz)
        outs = []
        for hd in range(nh):
            c0 = hd * HEAD_DIM
            outs.append(_rms(od_s[:, c0:c0 + HEAD_DIM], hnorm[:, c0:c0 + HEAD_DIM]))
        o_s[r0:r0 + CHUNK, 0:dh] = (jnp.concatenate(outs, axis=1) * sgz).astype(BF16)

    xr = proj_s[:, 4 * dh:4 * dh + dr]
    yr = proj_s[:, 4 * dh + dr:4 * dh + 2 * dr]
    ext_s[SUBLANES:, :] = xr
    xc = p512_ref[_P_CW0 + CONV_WIDTH - 1:_P_CW0 + CONV_WIDTH, :] * xr + p512_ref[_P_CB:_P_CB + 1, :]
    for k in range(1, CONV_WIDTH):
        wk = p512_ref[_P_CW0 + CONV_WIDTH - 1 - k:_P_CW0 + CONV_WIDTH - k, :]
        xc = xc + wk * ext_s[SUBLANES - k:SUBLANES - k + ts, :]
    ext_s[0:SUBLANES, :] = xr[ts - SUBLANES:, :]

    gates = _dot(xc.astype(BF16), w_ri_ref[...])
    r_g = _sigmoid(gates[:, 0:dr] + p512_ref[_P_BR:_P_BR + 1, :])
    i_g = _sigmoid(gates[:, dr:2 * dr] + p512_ref[_P_BI:_P_BI + 1, :])
    lam = p512_ref[_P_LAM:_P_LAM + 1, :]
    log_sig = jnp.minimum(lam, 0.0) - jnp.log(1.0 + jnp.exp(-jnp.abs(lam)))
    a_t = jnp.exp(RG_C * r_g * log_sig)
    y_t = 1.0 - a_t * a_t
    u_t = y_t * lax.rsqrt(jnp.maximum(y_t, TINY)) * (i_g * xc)
    n_tiles = ts // SUBLANES
    a3 = a_t.reshape(n_tiles, SUBLANES, dr)
    u3 = u_t.reshape(n_tiles, SUBLANES, dr)
    row3 = lax.broadcasted_iota(jnp.int32, (n_tiles, SUBLANES, dr), 1)
    d = 1
    while d < SUBLANES:
        keep = row3 >= d
        a_prev = jnp.where(keep, pltpu.roll(a3, d, 1), 1.0)
        u_prev = jnp.where(keep, pltpu.roll(u3, d, 1), 0.0)
        u3 = a3 * u_prev + u3
        a3 = a3 * a_prev
        d *= 2
    carry = rgc_s[SUBLANES - 1:SUBLANES, :]
    h_tiles = []
    for t in range(n_tiles):
        h_t = a3[t] * carry + u3[t]
        h_tiles.append(h_t)
        carry = h_t[SUBLANES - 1:SUBLANES, :]
    h_rg = jnp.concatenate(h_tiles, axis=0)
    rgc_s[...] = h_tiles[-1]
    gelu = 0.5 * yr * (1.0 + jnp.tanh(0.7978845608028654 * (yr + 0.044715 * (yr * yr * yr))))
    o_s[:, dh:dh + dr] = _rms(h_rg * gelu, p512_ref[_P_RNORM:_P_RNORM + 1, :]).astype(BF16)

    mix = _dot(o_s[...], w_out_ref[...])
    x1 = x + gt1 * _rms(mix, p1024_ref[_P_GPOST:_P_GPOST + 1, :])
    x1_ref[0] = x1
    h2 = _rms(x1, p1024_ref[_P_GFFN:_P_GFFN + 1, :] * (1.0 + sc2)) + sh2
    _store_row_tiles(h2_ref.at[0], _pack_rows(h2))

    h_hi, h_mid, _ = _split3(h2)
    w_hi, w_mid, _ = _split3(w_rt_ref[...])
    lg = _dot(h_hi, w_hi) + (_dot(h_mid, w_hi) + _dot(h_hi, w_mid)) + b_rt_ref[...]
    lane = lax.broadcasted_iota(jnp.int32, (ts, LANES), 1)
    lane_f = lane.astype(F32)
    neg = jnp.float32(-jnp.inf)
    big = jnp.float32(2 * LANES)
    is_g = jnp.logical_and(lane >= GROUP_LANE0, lane < GROUP_LANE0 + N_GROUPS)
    gl = jnp.where(is_g, lg, neg)
    g_max = jnp.max(gl, axis=1, keepdims=True)
    g_star = jnp.min(jnp.where(gl == g_max, lane_f, big), axis=1, keepdims=True) - float(GROUP_LANE0)
    gate_g = 1.0 / jnp.sum(jnp.exp(gl - g_max), axis=1, keepdims=True)
    grp_f = jnp.floor(lane_f * (1.0 / EXPERTS_PER_GROUP))
    el = jnp.where(jnp.logical_and(lane < N_EXPERTS, grp_f == g_star), lg, neg)
    m1 = jnp.max(el, axis=1, keepdims=True)
    i1 = jnp.min(jnp.where(el == m1, lane_f, big), axis=1, keepdims=True)
    el2 = jnp.where(lane_f == i1, neg, el)
    m2 = jnp.max(el2, axis=1, keepdims=True)
    i2 = jnp.min(jnp.where(el2 == m2, lane_f, big), axis=1, keepdims=True)
    e21 = jnp.exp(m2 - m1)
    w0 = gate_g / (1.0 + e21)
    w1 = gate_g * e21 / (1.0 + e21)

    hit0 = lane_f == i1
    hit1 = lane_f == i2
    oh = jnp.where(jnp.logical_or(hit0, hit1), 1.0, 0.0)
    before = _dot(ltri_ref[...], oh.astype(BF16)) + cntc_s[0:1, :]
    rank0 = jnp.sum(jnp.where(hit0, before, 0.0), axis=1, keepdims=True)
    rank1 = jnp.sum(jnp.where(hit1, before, 0.0), axis=1, keepdims=True)
    cntc_s[...] = cntc_s[...] + jnp.sum(oh, axis=0, keepdims=True)
    cnt_ref[...] = cntc_s[...]

    route = jnp.where(lane == _R_E0, i1, 0.0)
    route = jnp.where(lane == _R_E1, i2, route)
    route = jnp.where(lane == _R_W0, w0, route)
    route = jnp.where(lane == _R_W1, w1, route)
    route = jnp.where(lane == _R_RANK0, rank0, route)
    route = jnp.where(lane == _R_RANK1, rank1, route)
    route_ref[...] = route


def _mix(x, mod, p1024, p512, w_in, w_ri, w_out, w_rt, b_rt):
    bsz, seq, d = x.shape
    ts = TILE_S
    dh = p512.shape[1]
    dr = p512.shape[1]
    nh = dh // HEAD_DIM
    n_s = seq // ts
    assert CHUNK == HEAD_DIM
    hmat_np, hmask_np, dmask_np = _hier_constants(CHUNK)
    dmask = jnp.asarray(dmask_np, F32)
    hmat = jnp.asarray(hmat_np, BF16)
    hmask = jnp.asarray(hmask_np, F32)
    tt = np.arange(ts)
    ltri = jnp.asarray((tt[None, :] < tt[:, None]).astype(np.float32), BF16)

    def const(shape):
        return pl.BlockSpec(shape, lambda b, s: (0,) * len(shape))

    return pl.pallas_call(
        _mix_kernel,
        grid=(bsz, n_s),
        in_specs=[
            pl.BlockSpec((1, ts, d), lambda b, s: (b, s, 0)),
            pl.BlockSpec((1, mod.shape[1], d), lambda b, s: (b, 0, 0)),
            const(p1024.shape), const(p512.shape), const(w_in.shape), const(w_ri.shape),
            const(w_out.shape), const(w_rt.shape), const(b_rt.shape),
            const(hmat.shape), const(hmask.shape), const(dmask.shape), const(ltri.shape),
        ],
        out_specs=[
            pl.BlockSpec((1, ts, d), lambda b, s: (b, s, 0)),
            pl.BlockSpec((1, ts * ROW_TILES, LANES), lambda b, s: (b, s, 0)),
            pl.BlockSpec((ts, LANES), lambda b, s: (b * n_s + s, 0)),
            pl.BlockSpec((SUBLANES, LANES), lambda b, s: (0, 0)),
        ],
        out_shape=[
            jax.ShapeDtypeStruct((bsz, seq, d), F32),
            jax.ShapeDtypeStruct((bsz, seq * ROW_TILES, LANES), jnp.uint32),
            jax.ShapeDtypeStruct((bsz * seq, LANES), F32),
            jax.ShapeDtypeStruct((SUBLANES, LANES), F32),
        ],
        scratch_shapes=[
            pltpu.VMEM((ts, w_in.shape[1]), F32),
            pltpu.VMEM((CHUNK, dh), F32),
            pltpu.VMEM((CHUNK, dh), F32),
            pltpu.VMEM((CHUNK, dh), F32),
            pltpu.VMEM((CHUNK, dh), F32),
            pltpu.VMEM((hmat.shape[0], dh), F32),
            pltpu.VMEM((CHUNK, dh), F32),
            pltpu.VMEM((ts, dh + dr), BF16),
            pltpu.VMEM((nh, HEAD_DIM, HEAD_DIM), F32),
            pltpu.VMEM((SUBLANES, dr), F32),
            pltpu.VMEM((ts + SUBLANES, dr), F32),
            pltpu.VMEM((SUBLANES, LANES), F32),
        ],
        compiler_params=pltpu.CompilerParams(
            dimension_semantics=("arbitrary", "arbitrary"), vmem_limit_bytes=VMEM_LIMIT_BYTES),
        name="mix_route",
    )(x, mod, p1024, p512, w_in, w_ri, w_out, w_rt, b_rt, hmat, hmask, dmask, ltri)


def _dispatch_kernel(tail_ref, dest_ref, h2_ref, xb_ref, zero_s, sem, zsem):
    tt = h2_ref.shape[0]

    @pl.when(pl.program_id(0) == 0)
    def _():
        zero_s[...] = jnp.zeros_like(zero_s)

        def tail_copy(e):
            row0 = pl.multiple_of(jnp.maximum(tail_ref[e], 0), MOE_BLOCK)
            return pltpu.make_async_copy(zero_s, xb_ref.at[pl.ds(row0, MOE_BLOCK)], zsem)

        for e in range(tail_ref.shape[0]):
            @pl.when(tail_ref[e] >= 0)
            def _():
                tail_copy(e).start()
        for e in range(tail_ref.shape[0]):
            @pl.when(tail_ref[e] >= 0)
            def _():
                tail_copy(e).wait()

    def row_copy(t, slot):
        return pltpu.make_async_copy(h2_ref.at[pl.ds(t, 1)], xb_ref.at[pl.ds(slot, 1)], sem)

    def issue(t, carry):
        row_copy(t, dest_ref[0, 0, 2 * t]).start()
        row_copy(t, dest_ref[0, 0, 2 * t + 1]).start()
        return carry

    lax.fori_loop(0, tt, issue, 0, unroll=8)

    def drain(t, carry):
        row_copy(0, 0).wait()
        row_copy(0, 0).wait()
        return carry

    lax.fori_loop(0, tt, drain, 0, unroll=8)


def _dispatch(tails, dest, h2, n_slots):
    n_tok = h2.shape[0]
    row = h2.shape[1:]
    tt = dest.shape[2] // TOP_K
    grid_spec = pltpu.PrefetchScalarGridSpec(
        num_scalar_prefetch=1,
        grid=(n_tok // tt,),
        in_specs=[
            pl.BlockSpec((1, 1, TOP_K * tt), lambda i, tl: (i, 0, 0), memory_space=pltpu.SMEM),
            pl.BlockSpec((tt,) + row, lambda i, tl: (i, 0, 0)),
        ],
        out_specs=pl.BlockSpec(memory_space=pl.ANY),
        scratch_shapes=[pltpu.VMEM((MOE_BLOCK,) + row, h2.dtype),
                        pltpu.SemaphoreType.DMA(()), pltpu.SemaphoreType.DMA(())],
    )
    return pl.pallas_call(
        _dispatch_kernel,
        grid_spec=grid_spec,
        out_shape=jax.ShapeDtypeStruct((n_slots,) + row, h2.dtype),
        compiler_params=pltpu.CompilerParams(dimension_semantics=("arbitrary",)),
        name="dispatch_rows",
    )(tails, dest, h2)


def _ffn_kernel(blk_e_ref, nblk_ref, xb_ref, w1_ref, w3_ref, w2_ref, yb_ref, w1b, w3b, w2b):
    b = pl.program_id(0)
    prev = blk_e_ref[jnp.maximum(b - 1, 0)]
    fresh = jnp.logical_or(b == 0, blk_e_ref[b] != prev)
    live = b < nblk_ref[0]

    @pl.when(jnp.logical_and(fresh, live))
    def _():
        w1b[...] = w1_ref[0].astype(BF16)
        w3b[...] = w3_ref[0].astype(BF16)
        w2b[...] = w2_ref[0].astype(BF16)

    @pl.when(live)
    def _():
        xv = _unpack_rows(_load_row_tiles(xb_ref)).astype(BF16)
        a = _dot(xv, w1b[...])
        g = _dot(xv, w3b[...])
        hact = (a * _sigmoid(a) * g).astype(BF16)
        _store_row_tiles(yb_ref, _pack_rows(_dot(hact, w2b[...])))

    @pl.when(jnp.logical_not(live))
    def _():
        yb_ref[...] = jnp.zeros_like(yb_ref)


def _ffn(blk_e, nblk, xb, w1, w3, w2):
    n_rows, lanes = xb.shape
    blk_rows = MOE_BLOCK * ROW_TILES
    n_blocks = n_rows // blk_rows
    d, de = w1.shape[1], w1.shape[2]

    def x_map(b, be, nb):
        return (jnp.minimum(b, nb[0] - 1), 0)

    def w_map(b, be, nb):
        return (be[b], 0, 0)

    grid_spec = pltpu.PrefetchScalarGridSpec(
        num_scalar_prefetch=2,
        grid=(n_blocks,),
        in_specs=[
            pl.BlockSpec((blk_rows, lanes), x_map),
            pl.BlockSpec((1, d, de), w_map),
            pl.BlockSpec((1, d, de), w_map),
            pl.BlockSpec((1, de, d), w_map),
        ],
        out_specs=pl.BlockSpec((blk_rows, lanes), lambda b, be, nb: (b, 0)),
        scratch_shapes=[pltpu.VMEM((d, de), BF16), pltpu.VMEM((d, de), BF16), pltpu.VMEM((de, d), BF16)],
    )
    return pl.pallas_call(
        _ffn_kernel,
        grid_spec=grid_spec,
        out_shape=jax.ShapeDtypeStruct((n_rows, lanes), xb.dtype),
        compiler_params=pltpu.CompilerParams(
            dimension_semantics=("arbitrary",), vmem_limit_bytes=VMEM_LIMIT_BYTES),
        name="expert_ffn",
    )(blk_e, nblk, xb, w1, w3, w2)


def _combine_kernel(dest_ref, dest_next_ref, yb_ref, route_ref, x1_ref, mod_ref, g_ref, out_ref, ybuf, sem):
    tt = x1_ref.shape[0]
    i = pl.program_id(0)
    cur = i % 2

    def row_copy(slot, buf, k, t):
        return pltpu.make_async_copy(yb_ref.at[slot], ybuf.at[buf, k, pl.ds(t * ROW_TILES, ROW_TILES)], sem.at[buf])

    def issue(d_ref, buf):
        def body(t, carry):
            row_copy(d_ref[0, 0, 2 * t], buf, 0, t).start()
            row_copy(d_ref[0, 0, 2 * t + 1], buf, 1, t).start()
            return carry
        lax.fori_loop(0, tt, body, 0, unroll=8)

    @pl.when(i == 0)
    def _():
        issue(dest_ref, 0)

    @pl.when(i + 1 < pl.num_programs(0))
    def _():
        issue(dest_next_ref, 1 - cur)

    def drain(t, carry):
        row_copy(0, cur, 0, 0).wait()
        row_copy(0, cur, 0, 0).wait()
        return carry

    lax.fori_loop(0, tt, drain, 0, unroll=8)

    w0 = route_ref[:, _R_W0:_R_W0 + 1]
    w1 = route_ref[:, _R_W1:_R_W1 + 1]
    y = (_unpack_rows(_load_row_tiles(ybuf.at[cur, 0])) * w0
         + _unpack_rows(_load_row_tiles(ybuf.at[cur, 1])) * w1)
    gt2 = mod_ref[0, 5:6, :]
    out_ref[...] = x1_ref[...] + gt2 * _rms(y, g_ref[...])


def _combine(dest, yb, route, x1, mod, g_post_ffn, seq):
    n_tok, d = x1.shape
    tt = dest.shape[2] // TOP_K
    n_steps = n_tok // tt
    per_b = seq // tt
    return pl.pallas_call(
        _combine_kernel,
        grid=(n_steps,),
        in_specs=[
            pl.BlockSpec((1, 1, TOP_K * tt), lambda i: (i, 0, 0), memory_space=pltpu.SMEM),
            pl.BlockSpec((1, 1, TOP_K * tt), lambda i: (jnp.minimum(i + 1, n_steps - 1), 0, 0),
                         memory_space=pltpu.SMEM),
            pl.BlockSpec(memory_space=pl.ANY),
            pl.BlockSpec((tt, LANES), lambda i: (i, 0)),
            pl.BlockSpec((tt, d), lambda i: (i, 0)),
            pl.BlockSpec((1, mod.shape[1], d), lambda i: (i // per_b, 0, 0)),
            pl.BlockSpec((1, d), lambda i: (0, 0)),
        ],
        out_specs=pl.BlockSpec((tt, d), lambda i: (i, 0)),
        out_shape=jax.ShapeDtypeStruct((n_tok, d), F32),
        scratch_shapes=[pltpu.VMEM((2, TOP_K, tt * ROW_TILES, LANES), yb.dtype), pltpu.SemaphoreType.DMA((2,))],
        compiler_params=pltpu.CompilerParams(
            dimension_semantics=("arbitrary",), vmem_limit_bytes=VMEM_LIMIT_BYTES),
        name="combine_rows",
    )(dest, dest, yb, route, x1, mod, g_post_ffn)


def _block_diag(w):
    nb, bi, bo = w.shape
    eye = jnp.eye(nb, dtype=w.dtype)
    return (w[:, :, None, :] * eye[:, None, :, None]).reshape(nb * bi, nb * bo)


def kernel(x, c, w_ada, b_ada, g_pre_mix, g_post_mix, g_pre_ffn, g_post_ffn, w_in, hgrn_gamma, hgrn_norm_g, conv_w, conv_b, rg_w_r, rg_b_r, rg_w_i, rg_b_i, rg_lambda, rg_norm_g, w_out, w_router_group, b_router_group, w_router_expert, b_router_expert, w1, w3, w2):
    bsz, seq, d = x.shape
    depth = w_ada.shape[0]
    assert depth == 1 and hgrn_gamma.shape[0] == 2
    dh = hgrn_norm_g.shape[1]
    dr = rg_norm_g.shape[1]
    assert dh == dr and dh % HEAD_DIM == 0 and seq % TILE_S == 0 and TILE_S % CHUNK == 0
    assert w_router_expert.shape[2] == N_EXPERTS and w_router_group.shape[2] == N_GROUPS
    n_tok = bsz * seq

    c8 = jnp.pad(c, ((0, SUBLANES - bsz), (0, 0)))
    p512 = jnp.concatenate([
        hgrn_gamma, hgrn_norm_g, conv_w[0], conv_b, rg_b_r, rg_b_i, rg_lambda, rg_norm_g,
        jnp.zeros((4, dh), F32)], axis=0)
    p1024 = jnp.concatenate([g_pre_mix, g_post_mix, g_pre_ffn, jnp.zeros((5, d), F32)], axis=0)
    w_ri = jnp.concatenate([_block_diag(rg_w_r[0]), _block_diag(rg_w_i[0])], axis=1).astype(BF16)
    w_rt = jnp.concatenate([w_router_expert[0], w_router_group[0],
                            jnp.zeros((d, LANES - N_EXPERTS - N_GROUPS), F32)], axis=1)
    b_rt = jnp.concatenate([b_router_expert[0], b_router_group[0],
                            jnp.zeros((LANES - N_EXPERTS - N_GROUPS,), F32)])[None, :]

    mod = _ada(c8, w_ada[0], b_ada[0][None, :])[:bsz].reshape(bsz, 6, d)

    x1, h2, route, cnt = _mix(x, mod, p1024, p512, w_in[0].astype(BF16), w_ri, w_out[0].astype(BF16), w_rt, b_rt)

    counts = cnt[0, :N_EXPERTS].astype(jnp.int32)
    pcounts = (counts + MOE_BLOCK - 1) // MOE_BLOCK * MOE_BLOCK
    pends = jnp.cumsum(pcounts)
    pstarts = pends - pcounts
    n_blocks = -(-(n_tok * TOP_K) // MOE_BLOCK) + N_EXPERTS
    blk_start = jnp.arange(n_blocks, dtype=jnp.int32) * MOE_BLOCK
    blk_e = jnp.minimum(jnp.sum((pends[None, :] <= blk_start[:, None]).astype(jnp.int32), axis=1), N_EXPERTS - 1)
    nblk = (pends[-1:] // MOE_BLOCK).astype(jnp.int32)
    eids = route[:, _R_E0:_R_E1 + 1].astype(jnp.int32)
    ranks = route[:, _R_RANK0:_R_RANK1 + 1].astype(jnp.int32)
    is_e = eids[:, :, None] == jnp.arange(N_EXPERTS, dtype=jnp.int32)
    dest = jnp.sum(jnp.where(is_e, pstarts, 0), axis=2) + ranks
    spare = pends[-1] + jnp.arange(N_EXPERTS, dtype=jnp.int32) * MOE_BLOCK
    tails = jnp.concatenate([jnp.where(pcounts > 0, pends - MOE_BLOCK, -1),
                             jnp.where(spare < n_blocks * MOE_BLOCK, spare, -1)]).astype(jnp.int32)

    n_slots = n_blocks * MOE_BLOCK
    xb = _dispatch(tails, dest.reshape(n_tok // TILE_D, 1, TOP_K * TILE_D),
                   h2.reshape(n_tok, ROW_TILES, LANES), n_slots)
    yb = _ffn(blk_e, nblk, xb.reshape(n_slots * ROW_TILES, LANES), w1[0], w3[0], w2[0])
    out = _combine(dest.reshape(n_tok // TILE_C, 1, TOP_K * TILE_C), yb.reshape(n_slots, ROW_TILES, LANES), route,
                   x1.reshape(n_tok, d), mod, g_post_ffn, seq)
    return out.reshape(bsz, seq, d)
```

```python
import functools

import numpy as np
import jax
import jax.numpy as jnp
from jax import lax
from jax.experimental import pallas as pl
from jax.experimental.pallas import tpu as pltpu

F32 = jnp.float32
BF16 = jnp.bfloat16

LANES = 128
SUBLANES = 8
ROW_TILES = 4
VMEM_LIMIT_BYTES = 56 * 1024 * 1024

EPS = 1e-6
HEAD_DIM = 128
CHUNK = 128
SUB = 8
RG_C = 8.0
LOG2E = 1.4426950408889634
TINY = 1e-30
RG_BLOCKS = 8
CONV_WIDTH = 4
N_GROUPS = 4
EXPERTS_PER_GROUP = 8
N_EXPERTS = N_GROUPS * EXPERTS_PER_GROUP
TOP_K = 2
MOE_BLOCK = 256
TILE_S = 512
TILE_T = 1024
GROUP_LANE0 = N_EXPERTS


def _levels(chunk):
    out, b = [], chunk // 2
    while b >= SUB:
        out.append(b)
        b //= 2
    return out


def _hier_constants(chunk):
    t = np.arange(chunk)
    low = (t[None, :] <= t[:, None]).astype(np.float32)
    mats, masks = [low], []
    for b in _levels(chunk):
        ref = (t // (2 * b)) * 2 * b + b - 1
        mats.append(low - low[ref])
        same = (t[:, None] // (2 * b)) == (t[None, :] // (2 * b))
        upper = (t[:, None] % (2 * b)) >= b
        lower = (t[None, :] % (2 * b)) < b
        masks.append((same & upper & lower).astype(np.float32))
    direct = [((t[None, :] == t[:, None] - j) & ((t[:, None] % SUB) >= j)).astype(np.float32) for j in range(SUB)]
    return np.concatenate(mats, 0), np.stack(masks, 0), np.stack(direct, 0)


def _rms(v, g):
    return v * lax.rsqrt(jnp.mean(v * v, axis=-1, keepdims=True) + EPS) * g


def _sigmoid(v):
    return 0.5 * jnp.tanh(0.5 * v) + 0.5


def _dot(a, b):
    return jnp.dot(a, b, preferred_element_type=F32)


def _dot_nt(a, b):
    return lax.dot_general(a, b, (((1,), (1,)), ((), ())), preferred_element_type=F32)


def _pack_rows(v):
    half = v.shape[1] // 2
    vb = v.astype(BF16).astype(F32)
    hi = lax.bitcast_convert_type(vb[:, :half], jnp.uint32)
    lo = lax.bitcast_convert_type(vb[:, half:], jnp.uint32)
    return hi | lax.shift_right_logical(lo, jnp.uint32(16))


def _unpack_rows(w):
    hi = lax.bitcast_convert_type(w & jnp.uint32(0xFFFF0000), F32)
    lo = lax.bitcast_convert_type(lax.shift_left(w, jnp.uint32(16)), F32)
    return jnp.concatenate([hi, lo], axis=1)


def _store_row_tiles(ref, w):
    n = w.shape[0]
    for j in range(ROW_TILES):
        ref[pl.ds(j, n, stride=ROW_TILES), :] = w[:, j * LANES:(j + 1) * LANES]


def _load_row_tiles(ref):
    n = ref.shape[0] // ROW_TILES
    return jnp.concatenate([ref[pl.ds(j, n, stride=ROW_TILES), :] for j in range(ROW_TILES)], axis=1)


def _split3(v):
    hi = v.astype(BF16)
    r1 = v - hi.astype(F32)
    mid = r1.astype(BF16)
    lo = (r1 - mid.astype(F32)).astype(BF16)
    return hi, mid, lo


def _ada_kernel(c_ref, w_ref, b_ref, o_ref):
    c = c_ref[...]
    sc = c * _sigmoid(c)
    s_hi, s_mid, _ = _split3(sc)
    w = w_ref[...]
    w_hi, w_mid, _ = _split3(w)
    acc = _dot(s_hi, w_hi) + (_dot(s_mid, w_hi) + _dot(s_hi, w_mid))
    o_ref[...] = acc + b_ref[...]


def _ada(c8, w_ada, b_ada):
    d, n = w_ada.shape
    tn = 512
    return pl.pallas_call(
        _ada_kernel,
        grid=(n // tn,),
        in_specs=[pl.BlockSpec((SUBLANES, d), lambda j: (0, 0)),
                  pl.BlockSpec((d, tn), lambda j: (0, j)),
                  pl.BlockSpec((1, tn), lambda j: (0, j))],
        out_specs=pl.BlockSpec((SUBLANES, tn), lambda j: (0, j)),
        out_shape=jax.ShapeDtypeStruct((SUBLANES, n), F32),
        compiler_params=pltpu.CompilerParams(dimension_semantics=("arbitrary",)),
        name="ada_mod",
    )(c8, w_ada, b_ada)


_P_GAMMA0, _P_GAMMA1, _P_HNORM, _P_CW0, _P_CB, _P_BR, _P_BI, _P_LAM, _P_RNORM = 0, 1, 2, 3, 7, 8, 9, 10, 11
_P_GPRE, _P_GPOST, _P_GFFN = 0, 1, 2
_R_E0, _R_E1, _R_W0, _R_W1, _R_RANK0, _R_RANK1 = 0, 1, 2, 3, 4, 5


def _mix_kernel(x_ref, mod_ref, p1024_ref, p512_ref, w_in_ref, w_ri_ref, w_out_ref, w_rt_ref, b_rt_ref,
                hmat_ref, hmask_ref, dmask_ref, ltri_ref,
                x1_ref, h2_ref, route_ref, cnt_ref,
                proj_s, qs_s, kk_s, f_s, sd_s, ea_s, od_s, o_s, st_s, rgc_s, ext_s, cntc_s):
    ts = x_ref.shape[1]
    dh = qs_s.shape[1]
    dr = rgc_s.shape[1]
    nh = dh // HEAD_DIM
    n_lv = hmask_ref.shape[0]
    si = pl.program_id(1)

    @pl.when(si == 0)
    def _():
        st_s[...] = jnp.zeros_like(st_s)
        rgc_s[...] = jnp.zeros_like(rgc_s)
        ext_s[0:SUBLANES, :] = jnp.zeros((SUBLANES, dr), F32)

    @pl.when(jnp.logical_and(pl.program_id(0) == 0, si == 0))
    def _():
        cntc_s[...] = jnp.zeros_like(cntc_s)

    x = x_ref[0]
    sh1, sc1, gt1 = mod_ref[0, 0:1, :], mod_ref[0, 1:2, :], mod_ref[0, 2:3, :]
    sh2, sc2 = mod_ref[0, 3:4, :], mod_ref[0, 4:5, :]

    h = _rms(x, p1024_ref[_P_GPRE:_P_GPRE + 1, :] * (1.0 + sc1)) + sh1
    proj_s[...] = _dot(h.astype(BF16), w_in_ref[...])

    g0 = p512_ref[_P_GAMMA0:_P_GAMMA0 + 1, :]
    g1 = p512_ref[_P_GAMMA1:_P_GAMMA1 + 1, :]
    gm = jnp.maximum(g0, g1)
    e0 = jnp.exp(g0 - gm)
    lb = e0 / (e0 + jnp.exp(g1 - gm))
    hnorm = p512_ref[_P_HNORM:_P_HNORM + 1, :]

    for c in range(ts // CHUNK):
        r0 = c * CHUNK
        q = proj_s[r0:r0 + CHUNK, 0:dh]
        fz = proj_s[r0:r0 + CHUNK, dh:2 * dh]
        qs_s[...] = q * _sigmoid(q)
        ez = jnp.exp(-jnp.abs(fz))
        inv = 1.0 / (1.0 + ez)
        pos = fz >= 0.0
        sg = jnp.where(pos, inv, ez * inv)
        sn = jnp.where(pos, ez * inv, inv)
        f = lb + (1.0 - lb) * sg
        kk_s[...] = (1.0 - lb) * sn
        f_s[...] = f
        g = jnp.log(f) * LOG2E
        ghi = g.astype(BF16)
        glo = (g - ghi.astype(F32)).astype(BF16)
        ea2 = _dot(hmat_ref[...], jnp.concatenate([ghi, glo], axis=1))
        ea_s[...] = ea2[:, 0:dh] + ea2[:, dh:2 * dh]

        for r in range(CHUNK // SUB):
            r8 = r * SUB
            q8 = qs_s[r8:r8 + SUB, :]
            k8 = kk_s[r8:r8 + SUB, :]
            f8 = f_s[r8:r8 + SUB, :]
            qd = q8
            tile = None
            for j in range(SUB):
                if j > 0:
                    qd = qd * (f8 if j == 1 else pltpu.roll(f8, j - 1, 0))
                a = qd * (k8 if j == 0 else pltpu.roll(k8, j, 0))
                m = dmask_ref[j, r8:r8 + SUB, :]
                parts = [jnp.sum(a[:, hd * HEAD_DIM:(hd + 1) * HEAD_DIM], axis=1, keepdims=True) * m
                         for hd in range(nh)]
                contrib = jnp.concatenate(parts, axis=1)
                tile = contrib if tile is None else tile + contrib
            sd_s[r8:r8 + SUB, :] = tile

        for hd in range(nh):
            c0 = hd * HEAD_DIM
            qh = qs_s[:, c0:c0 + HEAD_DIM]
            kh = kk_s[:, c0:c0 + HEAD_DIM]
            vh = proj_s[r0:r0 + CHUNK, 2 * dh + c0:2 * dh + c0 + HEAD_DIM]
            b = ea_s[0:CHUNK, c0:c0 + HEAD_DIM]
            s_acc = sd_s[:, c0:c0 + CHUNK]
            for lv in range(n_lv):
                e_lv = ea_s[(lv + 1) * CHUNK:(lv + 2) * CHUNK, c0:c0 + HEAD_DIM]
                w_lv = jnp.exp2(jnp.minimum(e_lv, -e_lv))
                s_acc = s_acc + hmask_ref[lv] * _dot_nt((qh * w_lv).astype(BF16), (kh * w_lv).astype(BF16))
            o = _dot(s_acc.astype(BF16), vh.astype(BF16))
            st_t = st_s[hd]
            o = o + _dot_nt((qh * jnp.exp2(b)).astype(BF16), st_t.astype(BF16))
            b_last = ea_s[CHUNK - 1:CHUNK, c0:c0 + HEAD_DIM]
            kd = (kh * jnp.exp2(b_last - b)).astype(BF16)
            st_s[hd] = st_t * jnp.exp2(b_last) + _dot(vh.T.astype(BF16), kd)
            od_s[:, c0:c0 + HEAD_DIM] = o

        gz = proj_s[r0:r0 + CHUNK, 3 * dh:4 * dh]
        sgz = gz * _sigmoid(gz)
        outs = []
        for hd in range(nh):
            c0 = hd * HEAD_DIM
            outs.append(_rms(od_s[:, c0:c0 + HEAD_DIM], hnorm[:, c0:c0 + HEAD_DIM]))
        o_s[r0:r0 + CHUNK, 0:dh] = (jnp.concatenate(outs, axis=1) * sgz).astype(BF16)

    xr = proj_s[:, 4 * dh:4 * dh + dr]
    yr = proj_s[:, 4 * dh + dr:4 * dh + 2 * dr]
    ext_s[SUBLANES:, :] = xr
    xc = p512_ref[_P_CW0 + CONV_WIDTH - 1:_P_CW0 + CONV_WIDTH, :] * xr + p512_ref[_P_CB:_P_CB + 1, :]
    for k in range(1, CONV_WIDTH):
        wk = p512_ref[_P_CW0 + CONV_WIDTH - 1 - k:_P_CW0 + CONV_WIDTH - k, :]
        xc = xc + wk * ext_s[SUBLANES - k:SUBLANES - k + ts, :]
    ext_s[0:SUBLANES, :] = xr[ts - SUBLANES:, :]

    gates = _dot(xc.astype(BF16), w_ri_ref[...])
    r_g = _sigmoid(gates[:, 0:dr] + p512_ref[_P_BR:_P_BR + 1, :])
    i_g = _sigmoid(gates[:, dr:2 * dr] + p512_ref[_P_BI:_P_BI + 1, :])
    lam = p512_ref[_P_LAM:_P_LAM + 1, :]
    log_sig = jnp.minimum(lam, 0.0) - jnp.log(1.0 + jnp.exp(-jnp.abs(lam)))
    a_t = jnp.exp(RG_C * r_g * log_sig)
    y_t = 1.0 - a_t * a_t
    u_t = y_t * lax.rsqrt(jnp.maximum(y_t, TINY)) * (i_g * xc)
    n_tiles = ts // SUBLANES
    a3 = a_t.reshape(n_tiles, SUBLANES, dr)
    u3 = u_t.reshape(n_tiles, SUBLANES, dr)
    row3 = lax.broadcasted_iota(jnp.int32, (n_tiles, SUBLANES, dr), 1)
    d = 1
    while d < SUBLANES:
        keep = row3 >= d
        a_prev = jnp.where(keep, pltpu.roll(a3, d, 1), 1.0)
        u_prev = jnp.where(keep, pltpu.roll(u3, d, 1), 0.0)
        u3 = a3 * u_prev + u3
        a3 = a3 * a_prev
        d *= 2
    carry = rgc_s[SUBLANES - 1:SUBLANES, :]
    h_tiles = []
    for t in range(n_tiles):
        h_t = a3[t] * carry + u3[t]
        h_tiles.append(h_t)
        carry = h_t[SUBLANES - 1:SUBLANES, :]
    h_rg = jnp.concatenate(h_tiles, axis=0)
    rgc_s[...] = h_tiles[-1]
    gelu = 0.5 * yr * (1.0 + jnp.tanh(0.7978845608028654 * (yr + 0.044715 * (yr * yr * yr))))
    o_s[:, dh:dh + dr] = _rms(h_rg * gelu, p512_ref[_P_RNORM:_P_RNORM + 1, :]).astype(BF16)

    mix = _dot(o_s[...], w_out_ref[...])
    x1 = x + gt1 * _rms(mix, p1024_ref[_P_GPOST:_P_GPOST + 1, :])
    x1_ref[0] = x1
    h2 = _rms(x1, p1024_ref[_P_GFFN:_P_GFFN + 1, :] * (1.0 + sc2)) + sh2
    _store_row_tiles(h2_ref.at[0], _pack_rows(h2))

    h_hi, h_mid, _ = _split3(h2)
    w_hi, w_mid, _ = _split3(w_rt_ref[...])
    lg = _dot(h_hi, w_hi) + (_dot(h_mid, w_hi) + _dot(h_hi, w_mid)) + b_rt_ref[...]
    lane = lax.broadcasted_iota(jnp.int32, (ts, LANES), 1)
    lane_f = lane.astype(F32)
    neg = jnp.float32(-jnp.inf)
    big = jnp.float32(2 * LANES)
    is_g = jnp.logical_and(lane >= GROUP_LANE0, lane < GROUP_LANE0 + N_GROUPS)
    gl = jnp.where(is_g, lg, neg)
    g_max = jnp.max(gl, axis=1, keepdims=True)
    g_star = jnp.min(jnp.where(gl == g_max, lane_f, big), axis=1, keepdims=True) - float(GROUP_LANE0)
    gate_g = 1.0 / jnp.sum(jnp.exp(gl - g_max), axis=1, keepdims=True)
    grp_f = jnp.floor(lane_f * (1.0 / EXPERTS_PER_GROUP))
    el = jnp.where(jnp.logical_and(lane < N_EXPERTS, grp_f == g_star), lg, neg)
    m1 = jnp.max(el, axis=1, keepdims=True)
    i1 = jnp.min(jnp.where(el == m1, lane_f, big), axis=1, keepdims=True)
    el2 = jnp.where(lane_f == i1, neg, el)
    m2 = jnp.max(el2, axis=1, keepdims=True)
    i2 = jnp.min(jnp.where(el2 == m2, lane_f, big), axis=1, keepdims=True)
    e21 = jnp.exp(m2 - m1)
    w0 = gate_g / (1.0 + e21)
    w1 = gate_g * e21 / (1.0 + e21)

    hit0 = lane_f == i1
    hit1 = lane_f == i2
    oh = jnp.where(jnp.logical_or(hit0, hit1), 1.0, 0.0)
    before = _dot(ltri_ref[...], oh.astype(BF16)) + cntc_s[0:1, :]
    rank0 = jnp.sum(jnp.where(hit0, before, 0.0), axis=1, keepdims=True)
    rank1 = jnp.sum(jnp.where(hit1, before, 0.0), axis=1, keepdims=True)
    cntc_s[...] = cntc_s[...] + jnp.sum(oh, axis=0, keepdims=True)
    cnt_ref[...] = cntc_s[...]

    route = jnp.where(lane == _R_E0, i1, 0.0)
    route = jnp.where(lane == _R_E1, i2, route)
    route = jnp.where(lane == _R_W0, w0, route)
    route = jnp.where(lane == _R_W1, w1, route)
    route = jnp.where(lane == _R_RANK0, rank0, route)
    route = jnp.where(lane == _R_RANK1, rank1, route)
    route_ref[...] = route


def _mix(x, mod, p1024, p512, w_in, w_ri, w_out, w_rt, b_rt):
    bsz, seq, d = x.shape
    ts = TILE_S
    dh = p512.shape[1]
    dr = p512.shape[1]
    nh = dh // HEAD_DIM
    n_s = seq // ts
    assert CHUNK == HEAD_DIM
    hmat_np, hmask_np, dmask_np = _hier_constants(CHUNK)
    dmask = jnp.asarray(dmask_np, F32)
    hmat = jnp.asarray(hmat_np, BF16)
    hmask = jnp.asarray(hmask_np, F32)
    tt = np.arange(ts)
    ltri = jnp.asarray((tt[None, :] < tt[:, None]).astype(np.float32), BF16)

    def const(shape):
        return pl.BlockSpec(shape, lambda b, s: (0,) * len(shape))

    return pl.pallas_call(
        _mix_kernel,
        grid=(bsz, n_s),
        in_specs=[
            pl.BlockSpec((1, ts, d), lambda b, s: (b, s, 0)),
            pl.BlockSpec((1, mod.shape[1], d), lambda b, s: (b, 0, 0)),
            const(p1024.shape), const(p512.shape), const(w_in.shape), const(w_ri.shape),
            const(w_out.shape), const(w_rt.shape), const(b_rt.shape),
            const(hmat.shape), const(hmask.shape), const(dmask.shape), const(ltri.shape),
        ],
        out_specs=[
            pl.BlockSpec((1, ts, d), lambda b, s: (b, s, 0)),
            pl.BlockSpec((1, ts * ROW_TILES, LANES), lambda b, s: (b, s, 0)),
            pl.BlockSpec((ts, LANES), lambda b, s: (b * n_s + s, 0)),
            pl.BlockSpec((SUBLANES, LANES), lambda b, s: (0, 0)),
        ],
        out_shape=[
            jax.ShapeDtypeStruct((bsz, seq, d), F32),
            jax.ShapeDtypeStruct((bsz, seq * ROW_TILES, LANES), jnp.uint32),
            jax.ShapeDtypeStruct((bsz * seq, LANES), F32),
            jax.ShapeDtypeStruct((SUBLANES, LANES), F32),
        ],
        scratch_shapes=[
            pltpu.VMEM((ts, w_in.shape[1]), F32),
            pltpu.VMEM((CHUNK, dh), F32),
            pltpu.VMEM((CHUNK, dh), F32),
            pltpu.VMEM((CHUNK, dh), F32),
            pltpu.VMEM((CHUNK, dh), F32),
            pltpu.VMEM((hmat.shape[0], dh), F32),
            pltpu.VMEM((CHUNK, dh), F32),
            pltpu.VMEM((ts, dh + dr), BF16),
            pltpu.VMEM((nh, HEAD_DIM, HEAD_DIM), F32),
            pltpu.VMEM((SUBLANES, dr), F32),
            pltpu.VMEM((ts + SUBLANES, dr), F32),
            pltpu.VMEM((SUBLANES, LANES), F32),
        ],
        compiler_params=pltpu.CompilerParams(
            dimension_semantics=("arbitrary", "arbitrary"), vmem_limit_bytes=VMEM_LIMIT_BYTES),
        name="mix_route",
    )(x, mod, p1024, p512, w_in, w_ri, w_out, w_rt, b_rt, hmat, hmask, dmask, ltri)


def _dispatch_kernel(tail_ref, dest_ref, h2_ref, xb_ref, zero_s, sem, zsem):
    tt = h2_ref.shape[0]

    @pl.when(pl.program_id(0) == 0)
    def _():
        zero_s[...] = jnp.zeros_like(zero_s)

        def tail_copy(e):
            row0 = pl.multiple_of(jnp.maximum(tail_ref[e], 0), MOE_BLOCK)
            return pltpu.make_async_copy(zero_s, xb_ref.at[pl.ds(row0, MOE_BLOCK)], zsem)

        for e in range(tail_ref.shape[0]):
            @pl.when(tail_ref[e] >= 0)
            def _():
                tail_copy(e).start()
        for e in range(tail_ref.shape[0]):
            @pl.when(tail_ref[e] >= 0)
            def _():
                tail_copy(e).wait()

    def row_copy(t, slot):
        return pltpu.make_async_copy(h2_ref.at[pl.ds(t, 1)], xb_ref.at[pl.ds(slot, 1)], sem)

    def issue(t, carry):
        row_copy(t, dest_ref[0, 0, 2 * t]).start()
        row_copy(t, dest_ref[0, 0, 2 * t + 1]).start(priority=1)
        return carry

    lax.fori_loop(0, tt, issue, 0, unroll=8)

    def drain(t, carry):
        row_copy(0, 0).wait()
        row_copy(0, 0).wait()
        return carry

    lax.fori_loop(0, tt, drain, 0, unroll=8)


def _dispatch(tails, dest, h2, n_slots):
    n_tok = h2.shape[0]
    row = h2.shape[1:]
    tt = dest.shape[2] // TOP_K
    grid_spec = pltpu.PrefetchScalarGridSpec(
        num_scalar_prefetch=1,
        grid=(n_tok // tt,),
        in_specs=[
            pl.BlockSpec((1, 1, TOP_K * tt), lambda i, tl: (i, 0, 0), memory_space=pltpu.SMEM),
            pl.BlockSpec((tt,) + row, lambda i, tl: (i, 0, 0)),
        ],
        out_specs=pl.BlockSpec(memory_space=pl.ANY),
        scratch_shapes=[pltpu.VMEM((MOE_BLOCK,) + row, h2.dtype),
                        pltpu.SemaphoreType.DMA(()), pltpu.SemaphoreType.DMA(())],
    )
    return pl.pallas_call(
        _dispatch_kernel,
        grid_spec=grid_spec,
        out_shape=jax.ShapeDtypeStruct((n_slots,) + row, h2.dtype),
        compiler_params=pltpu.CompilerParams(dimension_semantics=("arbitrary",)),
        name="dispatch_rows",
    )(tails, dest, h2)


def _ffn_kernel(first_ref, nblk_ref, spare_ref, w1_ref, w3_ref, w2_ref, xb_ref, yb_ref,
                w1b, w3b, w2b, xbuf, ybuf, xsem, ysem, zsem):
    e = pl.program_id(0)
    blk_rows = xbuf.shape[1]
    nb = nblk_ref[e]
    row0 = first_ref[e] * blk_rows

    def block_rows(i):
        return pl.ds(pl.multiple_of(row0 + i * blk_rows, blk_rows), blk_rows)

    def x_copy(i, slot):
        return pltpu.make_async_copy(xb_ref.at[block_rows(i)], xbuf.at[slot], xsem.at[slot])

    def y_copy(i, slot):
        return pltpu.make_async_copy(ybuf.at[slot], yb_ref.at[block_rows(i)], ysem.at[slot])

    @pl.when(e == 0)
    def _():
        ybuf[0] = jnp.zeros(ybuf.shape[1:], ybuf.dtype)

        def spare_copy(j):
            r = pl.multiple_of(jnp.maximum(spare_ref[j], 0) * blk_rows, blk_rows)
            return pltpu.make_async_copy(ybuf.at[0], yb_ref.at[pl.ds(r, blk_rows)], zsem)

        for j in range(spare_ref.shape[0]):
            @pl.when(spare_ref[j] >= 0)
            def _():
                spare_copy(j).start()
        for j in range(spare_ref.shape[0]):
            @pl.when(spare_ref[j] >= 0)
            def _():
                spare_copy(j).wait()

    @pl.when(nb > 0)
    def _():
        x_copy(0, 0).start()
        w1b[...] = w1_ref[0].astype(BF16)
        w3b[...] = w3_ref[0].astype(BF16)
        w2b[...] = w2_ref[0].astype(BF16)

        def body(i, carry):
            slot = i % 2
            x_copy(i, slot).wait()

            @pl.when(i + 1 < nb)
            def _():
                x_copy(i + 1, 1 - slot).start()

            @pl.when(i >= 2)
            def _():
                y_copy(i - 2, slot).wait()

            xv = _unpack_rows(_load_row_tiles(xbuf.at[slot])).astype(BF16)
            a = _dot(xv, w1b[...])
            g = _dot(xv, w3b[...])
            hact = (a * _sigmoid(a) * g).astype(BF16)
            _store_row_tiles(ybuf.at[slot], _pack_rows(_dot(hact, w2b[...])))
            y_copy(i, slot).start()
            return carry

        lax.fori_loop(0, nb, body, 0)

        @pl.when(nb >= 2)
        def _():
            y_copy(nb - 2, nb % 2).wait()

        y_copy(nb - 1, (nb - 1) % 2).wait()


def _ffn(first_blk, n_blk, spare_blk, xb, w1, w3, w2):
    n_rows, lanes = xb.shape
    blk_rows = MOE_BLOCK * ROW_TILES
    n_exp, d, de = w1.shape

    def w_map(e, *_):
        return (e, 0, 0)

    grid_spec = pltpu.PrefetchScalarGridSpec(
        num_scalar_prefetch=3,
        grid=(n_exp,),
        in_specs=[
            pl.BlockSpec((1, d, de), w_map),
            pl.BlockSpec((1, d, de), w_map),
            pl.BlockSpec((1, de, d), w_map),
            pl.BlockSpec(memory_space=pl.ANY),
        ],
        out_specs=pl.BlockSpec(memory_space=pl.ANY),
        scratch_shapes=[pltpu.VMEM((d, de), BF16), pltpu.VMEM((d, de), BF16), pltpu.VMEM((de, d), BF16),
                        pltpu.VMEM((2, blk_rows, lanes), xb.dtype), pltpu.VMEM((2, blk_rows, lanes), xb.dtype),
                        pltpu.SemaphoreType.DMA((2,)), pltpu.SemaphoreType.DMA((2,)), pltpu.SemaphoreType.DMA(())],
    )
    return pl.pallas_call(
        _ffn_kernel,
        grid_spec=grid_spec,
        out_shape=jax.ShapeDtypeStruct((n_rows, lanes), xb.dtype),
        compiler_params=pltpu.CompilerParams(
            dimension_semantics=("arbitrary",), vmem_limit_bytes=VMEM_LIMIT_BYTES),
        name="expert_ffn",
    )(first_blk, n_blk, spare_blk, w1, w3, w2, xb)


def _combine_kernel(dest_ref, dest_next_ref, yb_ref, route_ref, x1_ref, mod_ref, g_ref, out_ref, ybuf, sem):
    tt = x1_ref.shape[0]
    i = pl.program_id(0)
    cur = i % 2

    def row_copy(slot, buf, k, t):
        return pltpu.make_async_copy(yb_ref.at[slot], ybuf.at[buf, k, pl.ds(t * ROW_TILES, ROW_TILES)], sem.at[buf])

    def issue(d_ref, buf):
        def body(t, carry):
            row_copy(d_ref[0, 0, 2 * t], buf, 0, t).start()
            row_copy(d_ref[0, 0, 2 * t + 1], buf, 1, t).start(priority=1)
            return carry
        lax.fori_loop(0, tt, body, 0, unroll=8)

    @pl.when(i == 0)
    def _():
        issue(dest_ref, 0)

    @pl.when(i + 1 < pl.num_programs(0))
    def _():
        issue(dest_next_ref, 1 - cur)

    def drain(t, carry):
        row_copy(0, cur, 0, 0).wait()
        row_copy(0, cur, 0, 0).wait()
        return carry

    lax.fori_loop(0, tt, drain, 0, unroll=8)

    w0 = route_ref[:, _R_W0:_R_W0 + 1]
    w1 = route_ref[:, _R_W1:_R_W1 + 1]
    y = (_unpack_rows(_load_row_tiles(ybuf.at[cur, 0])) * w0
         + _unpack_rows(_load_row_tiles(ybuf.at[cur, 1])) * w1)
    gt2 = mod_ref[0, 5:6, :]
    out_ref[...] = x1_ref[...] + gt2 * _rms(y, g_ref[...])


def _combine(dest, yb, route, x1, mod, g_post_ffn, seq):
    n_tok, d = x1.shape
    tt = dest.shape[2] // TOP_K
    n_steps = n_tok // tt
    per_b = seq // tt
    return pl.pallas_call(
        _combine_kernel,
        grid=(n_steps,),
        in_specs=[
            pl.BlockSpec((1, 1, TOP_K * tt), lambda i: (i, 0, 0), memory_space=pltpu.SMEM),
            pl.BlockSpec((1, 1, TOP_K * tt), lambda i: (jnp.minimum(i + 1, n_steps - 1), 0, 0),
                         memory_space=pltpu.SMEM),
            pl.BlockSpec(memory_space=pl.ANY),
            pl.BlockSpec((tt, LANES), lambda i: (i, 0)),
            pl.BlockSpec((tt, d), lambda i: (i, 0)),
            pl.BlockSpec((1, mod.shape[1], d), lambda i: (i // per_b, 0, 0)),
            pl.BlockSpec((1, d), lambda i: (0, 0)),
        ],
        out_specs=pl.BlockSpec((tt, d), lambda i: (i, 0)),
        out_shape=jax.ShapeDtypeStruct((n_tok, d), F32),
        scratch_shapes=[pltpu.VMEM((2, TOP_K, tt * ROW_TILES, LANES), yb.dtype), pltpu.SemaphoreType.DMA((2,))],
        compiler_params=pltpu.CompilerParams(
            dimension_semantics=("arbitrary",), vmem_limit_bytes=VMEM_LIMIT_BYTES),
        name="combine_rows",
    )(dest, dest, yb, route, x1, mod, g_post_ffn)


def _block_diag(w):
    nb, bi, bo = w.shape
    eye = jnp.eye(nb, dtype=w.dtype)
    return (w[:, :, None, :] * eye[:, None, :, None]).reshape(nb * bi, nb * bo)


def kernel(x, c, w_ada, b_ada, g_pre_mix, g_post_mix, g_pre_ffn, g_post_ffn, w_in, hgrn_gamma, hgrn_norm_g, conv_w, conv_b, rg_w_r, rg_b_r, rg_w_i, rg_b_i, rg_lambda, rg_norm_g, w_out, w_router_group, b_router_group, w_router_expert, b_router_expert, w1, w3, w2):
    bsz, seq, d = x.shape
    depth = w_ada.shape[0]
    assert depth == 1 and hgrn_gamma.shape[0] == 2
    dh = hgrn_norm_g.shape[1]
    dr = rg_norm_g.shape[1]
    assert dh == dr and dh % HEAD_DIM == 0 and seq % TILE_S == 0 and TILE_S % CHUNK == 0
    assert w_router_expert.shape[2] == N_EXPERTS and w_router_group.shape[2] == N_GROUPS
    n_tok = bsz * seq

    c8 = jnp.pad(c, ((0, SUBLANES - bsz), (0, 0)))
    p512 = jnp.concatenate([
        hgrn_gamma, hgrn_norm_g, conv_w[0], conv_b, rg_b_r, rg_b_i, rg_lambda, rg_norm_g,
        jnp.zeros((4, dh), F32)], axis=0)
    p1024 = jnp.concatenate([g_pre_mix, g_post_mix, g_pre_ffn, jnp.zeros((5, d), F32)], axis=0)
    w_ri = jnp.concatenate([_block_diag(rg_w_r[0]), _block_diag(rg_w_i[0])], axis=1).astype(BF16)
    w_rt = jnp.concatenate([w_router_expert[0], w_router_group[0],
                            jnp.zeros((d, LANES - N_EXPERTS - N_GROUPS), F32)], axis=1)
    b_rt = jnp.concatenate([b_router_expert[0], b_router_group[0],
                            jnp.zeros((LANES - N_EXPERTS - N_GROUPS,), F32)])[None, :]

    mod = _ada(c8, w_ada[0], b_ada[0][None, :])[:bsz].reshape(bsz, 6, d)

    x1, h2, route, cnt = _mix(x, mod, p1024, p512, w_in[0].astype(BF16), w_ri, w_out[0].astype(BF16), w_rt, b_rt)

    counts = cnt[0, :N_EXPERTS].astype(jnp.int32)
    pcounts = (counts + MOE_BLOCK - 1) // MOE_BLOCK * MOE_BLOCK
    pends = jnp.cumsum(pcounts)
    pstarts = pends - pcounts
    n_blocks = -(-(n_tok * TOP_K) // MOE_BLOCK) + N_EXPERTS
    eids = route[:, _R_E0:_R_E1 + 1].astype(jnp.int32)
    ranks = route[:, _R_RANK0:_R_RANK1 + 1].astype(jnp.int32)
    is_e = eids[:, :, None] == jnp.arange(N_EXPERTS, dtype=jnp.int32)
    dest = jnp.sum(jnp.where(is_e, pstarts, 0), axis=2) + ranks
    spare = pends[-1] + jnp.arange(N_EXPERTS, dtype=jnp.int32) * MOE_BLOCK
    tails = jnp.concatenate([jnp.where(pcounts > 0, pends - MOE_BLOCK, -1),
                             jnp.where(spare < n_blocks * MOE_BLOCK, spare, -1)]).astype(jnp.int32)

    n_slots = n_blocks * MOE_BLOCK
    dest_tiles = dest.reshape(n_tok // TILE_T, 1, TOP_K * TILE_T)
    xb = _dispatch(tails, dest_tiles, h2.reshape(n_tok, ROW_TILES, LANES), n_slots)
    spare_blk = jnp.where(spare < n_slots, spare // MOE_BLOCK, -1).astype(jnp.int32)
    yb = _ffn((pstarts // MOE_BLOCK).astype(jnp.int32), (pcounts // MOE_BLOCK).astype(jnp.int32), spare_blk,
              xb.reshape(n_slots * ROW_TILES, LANES), w1[0], w3[0], w2[0])
    out = _combine(dest_tiles, yb.reshape(n_slots, ROW_TILES, LANES), route,
                   x1.reshape(n_tok, d), mod, g_post_ffn, seq)
    return out.reshape(bsz, seq, d)
```

```python
import functools

import numpy as np
import jax
import jax.numpy as jnp
from jax import lax
from jax.experimental import pallas as pl
from jax.experimental.pallas import tpu as pltpu

F32 = jnp.float32
BF16 = jnp.bfloat16

LANES = 128
SUBLANES = 8
ROW_TILES = 4
VMEM_LIMIT_BYTES = 56 * 1024 * 1024

EPS = 1e-6
HEAD_DIM = 128
CHUNK = 128
SUB = 8
RG_C = 8.0
LOG2E = 1.4426950408889634
TINY = 1e-30
RG_BLOCKS = 8
CONV_WIDTH = 4
N_GROUPS = 4
EXPERTS_PER_GROUP = 8
N_EXPERTS = N_GROUPS * EXPERTS_PER_GROUP
TOP_K = 2
MOE_BLOCK = 256
FFN_GROUP = 4
TILE_S = 512
TILE_T = 1024
GROUP_LANE0 = N_EXPERTS


def _levels(chunk):
    out, b = [], chunk // 2
    while b >= SUB:
        out.append(b)
        b //= 2
    return out


def _hier_constants(chunk):
    t = np.arange(chunk)
    low = (t[None, :] <= t[:, None]).astype(np.float32)
    mats, masks = [low], []
    for b in _levels(chunk):
        ref = (t // (2 * b)) * 2 * b + b - 1
        mats.append(low - low[ref])
        same = (t[:, None] // (2 * b)) == (t[None, :] // (2 * b))
        upper = (t[:, None] % (2 * b)) >= b
        lower = (t[None, :] % (2 * b)) < b
        masks.append((same & upper & lower).astype(np.float32))
    direct = [((t[None, :] == t[:, None] - j) & ((t[:, None] % SUB) >= j)).astype(np.float32) for j in range(SUB)]
    return np.concatenate(mats, 0), np.stack(masks, 0), np.stack(direct, 0)


def _rms(v, g):
    return v * lax.rsqrt(jnp.mean(v * v, axis=-1, keepdims=True) + EPS) * g


def _sigmoid(v):
    return 0.5 * jnp.tanh(0.5 * v) + 0.5


def _dot(a, b):
    return jnp.dot(a, b, preferred_element_type=F32)


def _dot_nt(a, b):
    return lax.dot_general(a, b, (((1,), (1,)), ((), ())), preferred_element_type=F32)


def _pack_rows(v):
    half = v.shape[1] // 2
    vb = v.astype(BF16).astype(F32)
    hi = lax.bitcast_convert_type(vb[:, :half], jnp.uint32)
    lo = lax.bitcast_convert_type(vb[:, half:], jnp.uint32)
    return hi | lax.shift_right_logical(lo, jnp.uint32(16))


def _unpack_rows(w):
    hi = lax.bitcast_convert_type(w & jnp.uint32(0xFFFF0000), F32)
    lo = lax.bitcast_convert_type(lax.shift_left(w, jnp.uint32(16)), F32)
    return jnp.concatenate([hi, lo], axis=1)


def _store_row_tiles(ref, w):
    n = w.shape[0]
    for j in range(ROW_TILES):
        ref[pl.ds(j, n, stride=ROW_TILES), :] = w[:, j * LANES:(j + 1) * LANES]


def _load_row_tiles(ref):
    n = ref.shape[0] // ROW_TILES
    return jnp.concatenate([ref[pl.ds(j, n, stride=ROW_TILES), :] for j in range(ROW_TILES)], axis=1)


def _split3(v):
    hi = v.astype(BF16)
    r1 = v - hi.astype(F32)
    mid = r1.astype(BF16)
    lo = (r1 - mid.astype(F32)).astype(BF16)
    return hi, mid, lo


def _ada_kernel(c_ref, w_ref, b_ref, o_ref):
    c = c_ref[...]
    sc = c * _sigmoid(c)
    s_hi, s_mid, _ = _split3(sc)
    w = w_ref[...]
    w_hi, w_mid, _ = _split3(w)
    acc = _dot(s_hi, w_hi) + (_dot(s_mid, w_hi) + _dot(s_hi, w_mid))
    o_ref[...] = acc + b_ref[...]


def _ada(c8, w_ada, b_ada):
    d, n = w_ada.shape
    tn = 512
    return pl.pallas_call(
        _ada_kernel,
        grid=(n // tn,),
        in_specs=[pl.BlockSpec((SUBLANES, d), lambda j: (0, 0)),
                  pl.BlockSpec((d, tn), lambda j: (0, j)),
                  pl.BlockSpec((1, tn), lambda j: (0, j))],
        out_specs=pl.BlockSpec((SUBLANES, tn), lambda j: (0, j)),
        out_shape=jax.ShapeDtypeStruct((SUBLANES, n), F32),
        compiler_params=pltpu.CompilerParams(dimension_semantics=("arbitrary",)),
        name="ada_mod",
    )(c8, w_ada, b_ada)


_P_GAMMA0, _P_GAMMA1, _P_HNORM, _P_CW0, _P_CB, _P_BR, _P_BI, _P_LAM, _P_RNORM = 0, 1, 2, 3, 7, 8, 9, 10, 11
_P_GPRE, _P_GPOST, _P_GFFN = 0, 1, 2
_R_E0, _R_E1, _R_W0, _R_W1, _R_RANK0, _R_RANK1 = 0, 1, 2, 3, 4, 5


def _mix_kernel(x_ref, mod_ref, p1024_ref, p512_ref, w_in_ref, w_ri_ref, w_out_ref, w_rt_ref, b_rt_ref,
                hmat_ref, hmask_ref, dmask_ref, ltri_ref,
                x1_ref, h2_ref, route_ref, cnt_ref,
                proj_s, qs_s, kk_s, f_s, sd_s, ea_s, od_s, o_s, st_s, rgc_s, ext_s, cntc_s):
    ts = x_ref.shape[1]
    dh = qs_s.shape[1]
    dr = rgc_s.shape[1]
    nh = dh // HEAD_DIM
    n_lv = hmask_ref.shape[0]
    si = pl.program_id(1)

    @pl.when(si == 0)
    def _():
        st_s[...] = jnp.zeros_like(st_s)
        rgc_s[...] = jnp.zeros_like(rgc_s)
        ext_s[0:SUBLANES, :] = jnp.zeros((SUBLANES, dr), F32)

    @pl.when(jnp.logical_and(pl.program_id(0) == 0, si == 0))
    def _():
        cntc_s[...] = jnp.zeros_like(cntc_s)

    x = x_ref[0]
    sh1, sc1, gt1 = mod_ref[0, 0:1, :], mod_ref[0, 1:2, :], mod_ref[0, 2:3, :]
    sh2, sc2 = mod_ref[0, 3:4, :], mod_ref[0, 4:5, :]

    h = _rms(x, p1024_ref[_P_GPRE:_P_GPRE + 1, :] * (1.0 + sc1)) + sh1
    proj_s[...] = _dot(h.astype(BF16), w_in_ref[...])

    g0 = p512_ref[_P_GAMMA0:_P_GAMMA0 + 1, :]
    g1 = p512_ref[_P_GAMMA1:_P_GAMMA1 + 1, :]
    gm = jnp.maximum(g0, g1)
    e0 = jnp.exp(g0 - gm)
    lb = e0 / (e0 + jnp.exp(g1 - gm))
    hnorm = p512_ref[_P_HNORM:_P_HNORM + 1, :]

    for c in range(ts // CHUNK):
        r0 = c * CHUNK
        q = proj_s[r0:r0 + CHUNK, 0:dh]
        fz = proj_s[r0:r0 + CHUNK, dh:2 * dh]
        qs_s[...] = q * _sigmoid(q)
        ez = jnp.exp(-jnp.abs(fz))
        inv = 1.0 / (1.0 + ez)
        pos = fz >= 0.0
        sg = jnp.where(pos, inv, ez * inv)
        sn = jnp.where(pos, ez * inv, inv)
        f = lb + (1.0 - lb) * sg
        kk_s[...] = (1.0 - lb) * sn
        f_s[...] = f
        g = jnp.log(f) * LOG2E
        ghi = g.astype(BF16)
        glo = (g - ghi.astype(F32)).astype(BF16)
        ea2 = _dot(hmat_ref[...], jnp.concatenate([ghi, glo], axis=1))
        ea_s[...] = ea2[:, 0:dh] + ea2[:, dh:2 * dh]

        for r in range(CHUNK // SUB):
            r8 = r * SUB
            q8 = qs_s[r8:r8 + SUB, :]
            k8 = kk_s[r8:r8 + SUB, :]
            f8 = f_s[r8:r8 + SUB, :]
            qd = q8
            tile = None
            for j in range(SUB):
                if j > 0:
                    qd = qd * (f8 if j == 1 else pltpu.roll(f8, j - 1, 0))
                a = qd * (k8 if j == 0 else pltpu.roll(k8, j, 0))
                m = dmask_ref[j, r8:r8 + SUB, :]
                parts = [jnp.sum(a[:, hd * HEAD_DIM:(hd + 1) * HEAD_DIM], axis=1, keepdims=True) * m
                         for hd in range(nh)]
                contrib = jnp.concatenate(parts, axis=1)
                tile = contrib if tile is None else tile + contrib
            sd_s[r8:r8 + SUB, :] = tile

        for hd in range(nh):
            c0 = hd * HEAD_DIM
            qh = qs_s[:, c0:c0 + HEAD_DIM]
            kh = kk_s[:, c0:c0 + HEAD_DIM]
            vh = proj_s[r0:r0 + CHUNK, 2 * dh + c0:2 * dh + c0 + HEAD_DIM]
            b = ea_s[0:CHUNK, c0:c0 + HEAD_DIM]
            s_acc = sd_s[:, c0:c0 + CHUNK]
            for lv in range(n_lv):
                e_lv = ea_s[(lv + 1) * CHUNK:(lv + 2) * CHUNK, c0:c0 + HEAD_DIM]
                w_lv = jnp.exp2(jnp.minimum(e_lv, -e_lv))
                s_acc = s_acc + hmask_ref[lv] * _dot_nt((qh * w_lv).astype(BF16), (kh * w_lv).astype(BF16))
            o = _dot(s_acc.astype(BF16), vh.astype(BF16))
            st_t = st_s[hd]
            o = o + _dot_nt((qh * jnp.exp2(b)).astype(BF16), st_t.astype(BF16))
            b_last = ea_s[CHUNK - 1:CHUNK, c0:c0 + HEAD_DIM]
            kd = (kh * jnp.exp2(b_last - b)).astype(BF16)
            st_s[hd] = st_t * jnp.exp2(b_last) + _dot(vh.T.astype(BF16), kd)
            od_s[:, c0:c0 + HEAD_DIM] = o

        gz = proj_s[r0:r0 + CHUNK, 3 * dh:4 * dh]
        sgz = gz * _sigmoid(gz)
        outs = []
        for hd in range(nh):
            c0 = hd * HEAD_DIM
            outs.append(_rms(od_s[:, c0:c0 + HEAD_DIM], hnorm[:, c0:c0 + HEAD_DIM]))
        o_s[r0:r0 + CHUNK, 0:dh] = (jnp.concatenate(outs, axis=1) * sgz).astype(BF16)

    xr = proj_s[:, 4 * dh:4 * dh + dr]
    yr = proj_s[:, 4 * dh + dr:4 * dh + 2 * dr]
    ext_s[SUBLANES:, :] = xr
    xc = p512_ref[_P_CW0 + CONV_WIDTH - 1:_P_CW0 + CONV_WIDTH, :] * xr + p512_ref[_P_CB:_P_CB + 1, :]
    for k in range(1, CONV_WIDTH):
        wk = p512_ref[_P_CW0 + CONV_WIDTH - 1 - k:_P_CW0 + CONV_WIDTH - k, :]
        xc = xc + wk * ext_s[SUBLANES - k:SUBLANES - k + ts, :]
    ext_s[0:SUBLANES, :] = xr[ts - SUBLANES:, :]

    gates = _dot(xc.astype(BF16), w_ri_ref[...])
    r_g = _sigmoid(gates[:, 0:dr] + p512_ref[_P_BR:_P_BR + 1, :])
    i_g = _sigmoid(gates[:, dr:2 * dr] + p512_ref[_P_BI:_P_BI + 1, :])
    lam = p512_ref[_P_LAM:_P_LAM + 1, :]
    log_sig = jnp.minimum(lam, 0.0) - jnp.log(1.0 + jnp.exp(-jnp.abs(lam)))
    a_t = jnp.exp(RG_C * r_g * log_sig)
    y_t = 1.0 - a_t * a_t
    u_t = y_t * lax.rsqrt(jnp.maximum(y_t, TINY)) * (i_g * xc)
    n_tiles = ts // SUBLANES
    a3 = a_t.reshape(n_tiles, SUBLANES, dr)
    u3 = u_t.reshape(n_tiles, SUBLANES, dr)
    row3 = lax.broadcasted_iota(jnp.int32, (n_tiles, SUBLANES, dr), 1)
    d = 1
    while d < SUBLANES:
        keep = row3 >= d
        a_prev = jnp.where(keep, pltpu.roll(a3, d, 1), 1.0)
        u_prev = jnp.where(keep, pltpu.roll(u3, d, 1), 0.0)
        u3 = a3 * u_prev + u3
        a3 = a3 * a_prev
        d *= 2
    carry = rgc_s[SUBLANES - 1:SUBLANES, :]
    h_tiles = []
    for t in range(n_tiles):
        h_t = a3[t] * carry + u3[t]
        h_tiles.append(h_t)
        carry = h_t[SUBLANES - 1:SUBLANES, :]
    h_rg = jnp.concatenate(h_tiles, axis=0)
    rgc_s[...] = h_tiles[-1]
    gelu = 0.5 * yr * (1.0 + jnp.tanh(0.7978845608028654 * (yr + 0.044715 * (yr * yr * yr))))
    o_s[:, dh:dh + dr] = _rms(h_rg * gelu, p512_ref[_P_RNORM:_P_RNORM + 1, :]).astype(BF16)

    mix = _dot(o_s[...], w_out_ref[...])
    x1 = x + gt1 * _rms(mix, p1024_ref[_P_GPOST:_P_GPOST + 1, :])
    x1_ref[0] = x1
    h2 = _rms(x1, p1024_ref[_P_GFFN:_P_GFFN + 1, :] * (1.0 + sc2)) + sh2
    _store_row_tiles(h2_ref.at[0], _pack_rows(h2))

    h_hi, h_mid, _ = _split3(h2)
    w_hi, w_mid, _ = _split3(w_rt_ref[...])
    lg = _dot(h_hi, w_hi) + (_dot(h_mid, w_hi) + _dot(h_hi, w_mid)) + b_rt_ref[...]
    lane = lax.broadcasted_iota(jnp.int32, (ts, LANES), 1)
    lane_f = lane.astype(F32)
    neg = jnp.float32(-jnp.inf)
    big = jnp.float32(2 * LANES)
    is_g = jnp.logical_and(lane >= GROUP_LANE0, lane < GROUP_LANE0 + N_GROUPS)
    gl = jnp.where(is_g, lg, neg)
    g_max = jnp.max(gl, axis=1, keepdims=True)
    g_star = jnp.min(jnp.where(gl == g_max, lane_f, big), axis=1, keepdims=True) - float(GROUP_LANE0)
    gate_g = 1.0 / jnp.sum(jnp.exp(gl - g_max), axis=1, keepdims=True)
    grp_f = jnp.floor(lane_f * (1.0 / EXPERTS_PER_GROUP))
    el = jnp.where(jnp.logical_and(lane < N_EXPERTS, grp_f == g_star), lg, neg)
    m1 = jnp.max(el, axis=1, keepdims=True)
    i1 = jnp.min(jnp.where(el == m1, lane_f, big), axis=1, keepdims=True)
    el2 = jnp.where(lane_f == i1, neg, el)
    m2 = jnp.max(el2, axis=1, keepdims=True)
    i2 = jnp.min(jnp.where(el2 == m2, lane_f, big), axis=1, keepdims=True)
    e21 = jnp.exp(m2 - m1)
    w0 = gate_g / (1.0 + e21)
    w1 = gate_g * e21 / (1.0 + e21)

    hit0 = lane_f == i1
    hit1 = lane_f == i2
    oh = jnp.where(jnp.logical_or(hit0, hit1), 1.0, 0.0)
    before = _dot(ltri_ref[...], oh.astype(BF16)) + cntc_s[0:1, :]
    rank0 = jnp.sum(jnp.where(hit0, before, 0.0), axis=1, keepdims=True)
    rank1 = jnp.sum(jnp.where(hit1, before, 0.0), axis=1, keepdims=True)
    cntc_s[...] = cntc_s[...] + jnp.sum(oh, axis=0, keepdims=True)
    cnt_ref[...] = cntc_s[...]

    route = jnp.where(lane == _R_E0, i1, 0.0)
    route = jnp.where(lane == _R_E1, i2, route)
    route = jnp.where(lane == _R_W0, w0, route)
    route = jnp.where(lane == _R_W1, w1, route)
    route = jnp.where(lane == _R_RANK0, rank0, route)
    route = jnp.where(lane == _R_RANK1, rank1, route)
    route_ref[...] = route


def _mix(x, mod, p1024, p512, w_in, w_ri, w_out, w_rt, b_rt):
    bsz, seq, d = x.shape
    ts = TILE_S
    dh = p512.shape[1]
    dr = p512.shape[1]
    nh = dh // HEAD_DIM
    n_s = seq // ts
    assert CHUNK == HEAD_DIM
    hmat_np, hmask_np, dmask_np = _hier_constants(CHUNK)
    dmask = jnp.asarray(dmask_np, F32)
    hmat = jnp.asarray(hmat_np, BF16)
    hmask = jnp.asarray(hmask_np, F32)
    tt = np.arange(ts)
    ltri = jnp.asarray((tt[None, :] < tt[:, None]).astype(np.float32), BF16)

    def const(shape):
        return pl.BlockSpec(shape, lambda b, s: (0,) * len(shape))

    return pl.pallas_call(
        _mix_kernel,
        grid=(bsz, n_s),
        in_specs=[
            pl.BlockSpec((1, ts, d), lambda b, s: (b, s, 0)),
            pl.BlockSpec((1, mod.shape[1], d), lambda b, s: (b, 0, 0)),
            const(p1024.shape), const(p512.shape), const(w_in.shape), const(w_ri.shape),
            const(w_out.shape), const(w_rt.shape), const(b_rt.shape),
            const(hmat.shape), const(hmask.shape), const(dmask.shape), const(ltri.shape),
        ],
        out_specs=[
            pl.BlockSpec((1, ts, d), lambda b, s: (b, s, 0)),
            pl.BlockSpec((1, ts * ROW_TILES, LANES), lambda b, s: (b, s, 0)),
            pl.BlockSpec((ts, LANES), lambda b, s: (b * n_s + s, 0)),
            pl.BlockSpec((SUBLANES, LANES), lambda b, s: (0, 0)),
        ],
        out_shape=[
            jax.ShapeDtypeStruct((bsz, seq, d), F32),
            jax.ShapeDtypeStruct((bsz, seq * ROW_TILES, LANES), jnp.uint32),
            jax.ShapeDtypeStruct((bsz * seq, LANES), F32),
            jax.ShapeDtypeStruct((SUBLANES, LANES), F32),
        ],
        scratch_shapes=[
            pltpu.VMEM((ts, w_in.shape[1]), F32),
            pltpu.VMEM((CHUNK, dh), F32),
            pltpu.VMEM((CHUNK, dh), F32),
            pltpu.VMEM((CHUNK, dh), F32),
            pltpu.VMEM((CHUNK, dh), F32),
            pltpu.VMEM((hmat.shape[0], dh), F32),
            pltpu.VMEM((CHUNK, dh), F32),
            pltpu.VMEM((ts, dh + dr), BF16),
            pltpu.VMEM((nh, HEAD_DIM, HEAD_DIM), F32),
            pltpu.VMEM((SUBLANES, dr), F32),
            pltpu.VMEM((ts + SUBLANES, dr), F32),
            pltpu.VMEM((SUBLANES, LANES), F32),
        ],
        compiler_params=pltpu.CompilerParams(
            dimension_semantics=("arbitrary", "arbitrary"), vmem_limit_bytes=VMEM_LIMIT_BYTES),
        name="mix_route",
    )(x, mod, p1024, p512, w_in, w_ri, w_out, w_rt, b_rt, hmat, hmask, dmask, ltri)


def _dispatch_kernel(tail_ref, dest_ref, h2_ref, xb_ref, zero_s, sem, zsem):
    tt = h2_ref.shape[0]

    @pl.when(pl.program_id(0) == 0)
    def _():
        zero_s[...] = jnp.zeros_like(zero_s)

        def tail_copy(e):
            row0 = pl.multiple_of(jnp.maximum(tail_ref[e], 0), MOE_BLOCK)
            return pltpu.make_async_copy(zero_s, xb_ref.at[pl.ds(row0, MOE_BLOCK)], zsem)

        for e in range(tail_ref.shape[0]):
            @pl.when(tail_ref[e] >= 0)
            def _():
                tail_copy(e).start()
        for e in range(tail_ref.shape[0]):
            @pl.when(tail_ref[e] >= 0)
            def _():
                tail_copy(e).wait()

    def row_copy(t, slot):
        return pltpu.make_async_copy(h2_ref.at[pl.ds(t, 1)], xb_ref.at[pl.ds(slot, 1)], sem)

    def issue(t, carry):
        row_copy(t, dest_ref[0, 0, 2 * t]).start()
        row_copy(t, dest_ref[0, 0, 2 * t + 1]).start(priority=1)
        return carry

    lax.fori_loop(0, tt, issue, 0, unroll=8)

    def drain(t, carry):
        row_copy(0, 0).wait()
        row_copy(0, 0).wait()
        return carry

    lax.fori_loop(0, tt, drain, 0, unroll=8)


def _dispatch(tails, dest, h2, n_slots):
    n_tok = h2.shape[0]
    row = h2.shape[1:]
    tt = dest.shape[2] // TOP_K
    grid_spec = pltpu.PrefetchScalarGridSpec(
        num_scalar_prefetch=1,
        grid=(n_tok // tt,),
        in_specs=[
            pl.BlockSpec((1, 1, TOP_K * tt), lambda i, tl: (i, 0, 0), memory_space=pltpu.SMEM),
            pl.BlockSpec((tt,) + row, lambda i, tl: (i, 0, 0)),
        ],
        out_specs=pl.BlockSpec(memory_space=pl.ANY),
        scratch_shapes=[pltpu.VMEM((MOE_BLOCK,) + row, h2.dtype),
                        pltpu.SemaphoreType.DMA(()), pltpu.SemaphoreType.DMA(())],
    )
    return pl.pallas_call(
        _dispatch_kernel,
        grid_spec=grid_spec,
        out_shape=jax.ShapeDtypeStruct((n_slots,) + row, h2.dtype),
        compiler_params=pltpu.CompilerParams(dimension_semantics=("arbitrary",)),
        name="dispatch_rows",
    )(tails, dest, h2)


def _ffn_kernel(first_ref, nblk_ref, spare_ref, w1_ref, w3_ref, w2_ref, xb_ref, yb_ref,
                w1b, w3b, w2b, xbuf, ybuf, xt2, yt2, xt1, yt1, xsem, ysem, tsem, zsem):
    e = pl.program_id(0)
    blk_rows = xt1.shape[0]
    grp_rows = xbuf.shape[1]
    nb = nblk_ref[e]
    n_grp = nb // FFN_GROUP
    n_tail = nb - n_grp * FFN_GROUP
    row0 = first_ref[e] * blk_rows
    tail0 = row0 + n_grp * grp_rows

    def rows_at(start, n):
        return pl.ds(pl.multiple_of(start, blk_rows), n)

    def x_copy(i, slot):
        return pltpu.make_async_copy(xb_ref.at[rows_at(row0 + i * grp_rows, grp_rows)], xbuf.at[slot], xsem.at[slot])

    def y_copy(i, slot):
        return pltpu.make_async_copy(ybuf.at[slot], yb_ref.at[rows_at(row0 + i * grp_rows, grp_rows)], ysem.at[slot])

    def tail_piece(p, off_blocks):
        x_t, y_t = tails[p]
        hbm_rows = rows_at(tail0 + off_blocks * blk_rows, x_t.shape[0])
        return (pltpu.make_async_copy(xb_ref.at[hbm_rows], x_t, tsem.at[p, 0]),
                pltpu.make_async_copy(y_t, yb_ref.at[hbm_rows], tsem.at[p, 1]))

    def expert_rows(x_view, y_view):
        xv = _unpack_rows(_load_row_tiles(x_view)).astype(BF16)
        a = _dot(xv, w1b[...])
        g = _dot(xv, w3b[...])
        hact = (a * _sigmoid(a) * g).astype(BF16)
        _store_row_tiles(y_view, _pack_rows(_dot(hact, w2b[...])))

    @pl.when(e == 0)
    def _():
        yt1[...] = jnp.zeros_like(yt1)

        def spare_copy(j):
            r = pl.multiple_of(jnp.maximum(spare_ref[j], 0) * blk_rows, blk_rows)
            return pltpu.make_async_copy(yt1, yb_ref.at[pl.ds(r, blk_rows)], zsem)

        for j in range(spare_ref.shape[0]):
            @pl.when(spare_ref[j] >= 0)
            def _():
                spare_copy(j).start()
        for j in range(spare_ref.shape[0]):
            @pl.when(spare_ref[j] >= 0)
            def _():
                spare_copy(j).wait()

    tails = ((xt2, yt2), (xt1, yt1))
    pieces = ((0, 0, n_tail >= 2), (1, n_tail - n_tail % 2, n_tail % 2 == 1))

    @pl.when(nb > 0)
    def _():
        @pl.when(n_grp > 0)
        def _():
            x_copy(0, 0).start()

        for p, off, present in pieces:
            @pl.when(present)
            def _():
                tail_piece(p, off)[0].start()

        w1b[...] = w1_ref[0].astype(BF16)
        w3b[...] = w3_ref[0].astype(BF16)
        w2b[...] = w2_ref[0].astype(BF16)

        def body(i, carry):
            slot = i % 2
            x_copy(i, slot).wait()

            @pl.when(i + 1 < n_grp)
            def _():
                x_copy(i + 1, 1 - slot).start()

            @pl.when(i >= 2)
            def _():
                y_copy(i - 2, slot).wait()

            expert_rows(xbuf.at[slot], ybuf.at[slot])
            y_copy(i, slot).start()
            return carry

        lax.fori_loop(0, n_grp, body, 0)

        for p, off, present in pieces:
            @pl.when(present)
            def _():
                x_cp, y_cp = tail_piece(p, off)
                x_cp.wait()
                expert_rows(*tails[p])
                y_cp.start()

        @pl.when(n_grp >= 2)
        def _():
            y_copy(n_grp - 2, n_grp % 2).wait()

        @pl.when(n_grp >= 1)
        def _():
            y_copy(n_grp - 1, (n_grp - 1) % 2).wait()

        for p, off, present in pieces:
            @pl.when(present)
            def _():
                tail_piece(p, off)[1].wait()


def _ffn(first_blk, n_blk, spare_blk, xb, w1, w3, w2):
    n_rows, lanes = xb.shape
    blk_rows = MOE_BLOCK * ROW_TILES
    n_exp, d, de = w1.shape
    assert FFN_GROUP == 4

    def rows_buf(*lead):
        return pltpu.VMEM(lead + (lanes,), xb.dtype)

    def w_map(e, *_):
        return (e, 0, 0)

    grid_spec = pltpu.PrefetchScalarGridSpec(
        num_scalar_prefetch=3,
        grid=(n_exp,),
        in_specs=[
            pl.BlockSpec((1, d, de), w_map),
            pl.BlockSpec((1, d, de), w_map),
            pl.BlockSpec((1, de, d), w_map),
            pl.BlockSpec(memory_space=pl.ANY),
        ],
        out_specs=pl.BlockSpec(memory_space=pl.ANY),
        scratch_shapes=[pltpu.VMEM((d, de), BF16), pltpu.VMEM((d, de), BF16), pltpu.VMEM((de, d), BF16),
                        rows_buf(2, FFN_GROUP * blk_rows), rows_buf(2, FFN_GROUP * blk_rows),
                        rows_buf(2 * blk_rows), rows_buf(2 * blk_rows), rows_buf(blk_rows), rows_buf(blk_rows),
                        pltpu.SemaphoreType.DMA((2,)), pltpu.SemaphoreType.DMA((2,)),
                        pltpu.SemaphoreType.DMA((2, 2)), pltpu.SemaphoreType.DMA(())],
    )
    return pl.pallas_call(
        _ffn_kernel,
        grid_spec=grid_spec,
        out_shape=jax.ShapeDtypeStruct((n_rows, lanes), xb.dtype),
        compiler_params=pltpu.CompilerParams(
            dimension_semantics=("arbitrary",), vmem_limit_bytes=VMEM_LIMIT_BYTES),
        name="expert_ffn",
    )(first_blk, n_blk, spare_blk, w1, w3, w2, xb)


def _combine_kernel(dest_ref, dest_next_ref, yb_ref, route_ref, x1_ref, mod_ref, g_ref, out_ref, ybuf, sem):
    tt = x1_ref.shape[0]
    i = pl.program_id(0)
    cur = i % 2

    def row_copy(slot, buf, k, t):
        return pltpu.make_async_copy(yb_ref.at[slot], ybuf.at[buf, k, pl.ds(t * ROW_TILES, ROW_TILES)], sem.at[buf])

    def issue(d_ref, buf):
        def body(t, carry):
            row_copy(d_ref[0, 0, 2 * t], buf, 0, t).start()
            row_copy(d_ref[0, 0, 2 * t + 1], buf, 1, t).start(priority=1)
            return carry
        lax.fori_loop(0, tt, body, 0, unroll=8)

    @pl.when(i == 0)
    def _():
        issue(dest_ref, 0)

    @pl.when(i + 1 < pl.num_programs(0))
    def _():
        issue(dest_next_ref, 1 - cur)

    def drain(t, carry):
        row_copy(0, cur, 0, 0).wait()
        row_copy(0, cur, 0, 0).wait()
        return carry

    lax.fori_loop(0, tt, drain, 0, unroll=8)

    w0 = route_ref[:, _R_W0:_R_W0 + 1]
    w1 = route_ref[:, _R_W1:_R_W1 + 1]
    y = (_unpack_rows(_load_row_tiles(ybuf.at[cur, 0])) * w0
         + _unpack_rows(_load_row_tiles(ybuf.at[cur, 1])) * w1)
    gt2 = mod_ref[0, 5:6, :]
    out_ref[...] = x1_ref[...] + gt2 * _rms(y, g_ref[...])


def _combine(dest, yb, route, x1, mod, g_post_ffn, seq):
    n_tok, d = x1.shape
    tt = dest.shape[2] // TOP_K
    n_steps = n_tok // tt
    per_b = seq // tt
    return pl.pallas_call(
        _combine_kernel,
        grid=(n_steps,),
        in_specs=[
            pl.BlockSpec((1, 1, TOP_K * tt), lambda i: (i, 0, 0), memory_space=pltpu.SMEM),
            pl.BlockSpec((1, 1, TOP_K * tt), lambda i: (jnp.minimum(i + 1, n_steps - 1), 0, 0),
                         memory_space=pltpu.SMEM),
            pl.BlockSpec(memory_space=pl.ANY),
            pl.BlockSpec((tt, LANES), lambda i: (i, 0)),
            pl.BlockSpec((tt, d), lambda i: (i, 0)),
            pl.BlockSpec((1, mod.shape[1], d), lambda i: (i // per_b, 0, 0)),
            pl.BlockSpec((1, d), lambda i: (0, 0)),
        ],
        out_specs=pl.BlockSpec((tt, d), lambda i: (i, 0)),
        out_shape=jax.ShapeDtypeStruct((n_tok, d), F32),
        scratch_shapes=[pltpu.VMEM((2, TOP_K, tt * ROW_TILES, LANES), yb.dtype), pltpu.SemaphoreType.DMA((2,))],
        compiler_params=pltpu.CompilerParams(
            dimension_semantics=("arbitrary",), vmem_limit_bytes=VMEM_LIMIT_BYTES),
        name="combine_rows",
    )(dest, dest, yb, route, x1, mod, g_post_ffn)


def _block_diag(w):
    nb, bi, bo = w.shape
    eye = jnp.eye(nb, dtype=w.dtype)
    return (w[:, :, None, :] * eye[:, None, :, None]).reshape(nb * bi, nb * bo)


def kernel(x, c, w_ada, b_ada, g_pre_mix, g_post_mix, g_pre_ffn, g_post_ffn, w_in, hgrn_gamma, hgrn_norm_g, conv_w, conv_b, rg_w_r, rg_b_r, rg_w_i, rg_b_i, rg_lambda, rg_norm_g, w_out, w_router_group, b_router_group, w_router_expert, b_router_expert, w1, w3, w2):
    bsz, seq, d = x.shape
    depth = w_ada.shape[0]
    assert depth == 1 and hgrn_gamma.shape[0] == 2
    dh = hgrn_norm_g.shape[1]
    dr = rg_norm_g.shape[1]
    assert dh == dr and dh % HEAD_DIM == 0 and seq % TILE_S == 0 and TILE_S % CHUNK == 0
    assert w_router_expert.shape[2] == N_EXPERTS and w_router_group.shape[2] == N_GROUPS
    n_tok = bsz * seq

    c8 = jnp.pad(c, ((0, SUBLANES - bsz), (0, 0)))
    p512 = jnp.concatenate([
        hgrn_gamma, hgrn_norm_g, conv_w[0], conv_b, rg_b_r, rg_b_i, rg_lambda, rg_norm_g,
        jnp.zeros((4, dh), F32)], axis=0)
    p1024 = jnp.concatenate([g_pre_mix, g_post_mix, g_pre_ffn, jnp.zeros((5, d), F32)], axis=0)
    w_ri = jnp.concatenate([_block_diag(rg_w_r[0]), _block_diag(rg_w_i[0])], axis=1).astype(BF16)
    w_rt = jnp.concatenate([w_router_expert[0], w_router_group[0],
                            jnp.zeros((d, LANES - N_EXPERTS - N_GROUPS), F32)], axis=1)
    b_rt = jnp.concatenate([b_router_expert[0], b_router_group[0],
                            jnp.zeros((LANES - N_EXPERTS - N_GROUPS,), F32)])[None, :]

    mod = _ada(c8, w_ada[0], b_ada[0][None, :])[:bsz].reshape(bsz, 6, d)

    x1, h2, route, cnt = _mix(x, mod, p1024, p512, w_in[0].astype(BF16), w_ri, w_out[0].astype(BF16), w_rt, b_rt)

    counts = cnt[0, :N_EXPERTS].astype(jnp.int32)
    pcounts = (counts + MOE_BLOCK - 1) // MOE_BLOCK * MOE_BLOCK
    pends = jnp.cumsum(pcounts)
    pstarts = pends - pcounts
    n_blocks = -(-(n_tok * TOP_K) // MOE_BLOCK) + N_EXPERTS
    eids = route[:, _R_E0:_R_E1 + 1].astype(jnp.int32)
    ranks = route[:, _R_RANK0:_R_RANK1 + 1].astype(jnp.int32)
    is_e = eids[:, :, None] == jnp.arange(N_EXPERTS, dtype=jnp.int32)
    dest = jnp.sum(jnp.where(is_e, pstarts, 0), axis=2) + ranks
    spare = pends[-1] + jnp.arange(N_EXPERTS, dtype=jnp.int32) * MOE_BLOCK
    tails = jnp.concatenate([jnp.where(pcounts > 0, pends - MOE_BLOCK, -1),
                             jnp.where(spare < n_blocks * MOE_BLOCK, spare, -1)]).astype(jnp.int32)

    n_slots = n_blocks * MOE_BLOCK
    dest_tiles = dest.reshape(n_tok // TILE_T, 1, TOP_K * TILE_T)
    xb = _dispatch(tails, dest_tiles, h2.reshape(n_tok, ROW_TILES, LANES), n_slots)
    spare_blk = jnp.where(spare < n_slots, spare // MOE_BLOCK, -1).astype(jnp.int32)
    yb = _ffn((pstarts // MOE_BLOCK).astype(jnp.int32), (pcounts // MOE_BLOCK).astype(jnp.int32), spare_blk,
              xb.reshape(n_slots * ROW_TILES, LANES), w1[0], w3[0], w2[0])
    out = _combine(dest_tiles, yb.reshape(n_slots, ROW_TILES, LANES), route,
                   x1.reshape(n_tok, d), mod, g_post_ffn, seq)
    return out.reshape(bsz, seq, d)
```

```python
import functools

import numpy as np
import jax
import jax.numpy as jnp
from jax import lax
from jax.experimental import pallas as pl
from jax.experimental.pallas import tpu as pltpu

F32 = jnp.float32
BF16 = jnp.bfloat16

LANES = 128
SUBLANES = 8
ROW_TILES = 4
VMEM_LIMIT_BYTES = 56 * 1024 * 1024

EPS = 1e-6
HEAD_DIM = 128
CHUNK = 128
SUB = 8
RG_C = 8.0
LOG2E = 1.4426950408889634
TINY = 1e-30
RG_BLOCKS = 8
CONV_WIDTH = 4
N_GROUPS = 4
EXPERTS_PER_GROUP = 8
N_EXPERTS = N_GROUPS * EXPERTS_PER_GROUP
TOP_K = 2
MOE_BLOCK = 256
FFN_GROUP = 4
TILE_S = 512
TILE_T = 1024
GROUP_LANE0 = N_EXPERTS


def _levels(chunk):
    out, b = [], chunk // 2
    while b >= SUB:
        out.append(b)
        b //= 2
    return out


def _hier_constants(chunk):
    t = np.arange(chunk)
    low = (t[None, :] <= t[:, None]).astype(np.float32)
    masks = []
    for b in _levels(chunk):
        same = (t[:, None] // (2 * b)) == (t[None, :] // (2 * b))
        upper = (t[:, None] % (2 * b)) >= b
        lower = (t[None, :] % (2 * b)) < b
        masks.append(np.tile((same & upper & lower).astype(np.float32), (1, 2)))
    direct = [((t[None, :] == t[:, None] - j) & ((t[:, None] % SUB) >= j)).astype(np.float32) for j in range(SUB)]
    return low, np.stack(masks, 0), np.stack(direct, 0)


def _rms(v, g):
    return v * lax.rsqrt(jnp.mean(v * v, axis=-1, keepdims=True) + EPS) * g


def _sigmoid(v):
    return 0.5 * jnp.tanh(0.5 * v) + 0.5


def _dot(a, b):
    return jnp.dot(a, b, preferred_element_type=F32)


def _dot_nt(a, b):
    return lax.dot_general(a, b, (((1,), (1,)), ((), ())), preferred_element_type=F32)


def _pack_rows(v):
    half = v.shape[1] // 2
    vb = v.astype(BF16).astype(F32)
    hi = lax.bitcast_convert_type(vb[:, :half], jnp.uint32)
    lo = lax.bitcast_convert_type(vb[:, half:], jnp.uint32)
    return hi | lax.shift_right_logical(lo, jnp.uint32(16))


def _unpack_rows(w):
    hi = lax.bitcast_convert_type(w & jnp.uint32(0xFFFF0000), F32)
    lo = lax.bitcast_convert_type(lax.shift_left(w, jnp.uint32(16)), F32)
    return jnp.concatenate([hi, lo], axis=1)


def _store_row_tiles(ref, w):
    n = w.shape[0]
    for j in range(ROW_TILES):
        ref[pl.ds(j, n, stride=ROW_TILES), :] = w[:, j * LANES:(j + 1) * LANES]


def _load_row_tiles(ref):
    n = ref.shape[0] // ROW_TILES
    return jnp.concatenate([ref[pl.ds(j, n, stride=ROW_TILES), :] for j in range(ROW_TILES)], axis=1)


def _split3(v):
    hi = v.astype(BF16)
    r1 = v - hi.astype(F32)
    mid = r1.astype(BF16)
    lo = (r1 - mid.astype(F32)).astype(BF16)
    return hi, mid, lo


def _ada_kernel(c_ref, w_ref, b_ref, o_ref):
    c = c_ref[...]
    sc = c * _sigmoid(c)
    s_hi, s_mid, _ = _split3(sc)
    w = w_ref[...]
    w_hi, w_mid, _ = _split3(w)
    acc = _dot(s_hi, w_hi) + (_dot(s_mid, w_hi) + _dot(s_hi, w_mid))
    o_ref[...] = acc + b_ref[...]


def _ada(c8, w_ada, b_ada):
    d, n = w_ada.shape
    tn = 512
    return pl.pallas_call(
        _ada_kernel,
        grid=(n // tn,),
        in_specs=[pl.BlockSpec((SUBLANES, d), lambda j: (0, 0)),
                  pl.BlockSpec((d, tn), lambda j: (0, j)),
                  pl.BlockSpec((1, tn), lambda j: (0, j))],
        out_specs=pl.BlockSpec((SUBLANES, tn), lambda j: (0, j)),
        out_shape=jax.ShapeDtypeStruct((SUBLANES, n), F32),
        compiler_params=pltpu.CompilerParams(dimension_semantics=("arbitrary",)),
        name="ada_mod",
    )(c8, w_ada, b_ada)


_P_GAMMA0, _P_GAMMA1, _P_HNORM, _P_CW0, _P_CB, _P_BR, _P_BI, _P_LAM, _P_RNORM = 0, 1, 2, 3, 7, 8, 9, 10, 11
_P_GPRE, _P_GPOST, _P_GFFN = 0, 1, 2
_R_E0, _R_E1, _R_W0, _R_W1, _R_RANK0, _R_RANK1 = 0, 1, 2, 3, 4, 5


def _mix_kernel(x_ref, mod_ref, p1024_ref, p512_ref, w_in_ref, w_ri_ref, w_out_ref, w_rt_ref, b_rt_ref,
                hmat_ref, hmask_ref, dmask_ref, ltri_ref,
                x1_ref, h2_ref, route_ref, cnt_ref,
                proj_s, qs_s, kk_s, f_s, sd_s, ea_s, od_s, o_s, st_s, rgc_s, ext_s, cntc_s):
    ts = x_ref.shape[1]
    dh = qs_s.shape[1]
    dr = rgc_s.shape[1]
    nh = dh // HEAD_DIM
    n_lv = hmask_ref.shape[0]
    si = pl.program_id(1)

    @pl.when(si == 0)
    def _():
        st_s[...] = jnp.zeros_like(st_s)
        rgc_s[...] = jnp.zeros_like(rgc_s)
        ext_s[0:SUBLANES, :] = jnp.zeros((SUBLANES, dr), F32)

    @pl.when(jnp.logical_and(pl.program_id(0) == 0, si == 0))
    def _():
        cntc_s[...] = jnp.zeros_like(cntc_s)

    x = x_ref[0]
    sh1, sc1, gt1 = mod_ref[0, 0:1, :], mod_ref[0, 1:2, :], mod_ref[0, 2:3, :]
    sh2, sc2 = mod_ref[0, 3:4, :], mod_ref[0, 4:5, :]

    h = _rms(x, p1024_ref[_P_GPRE:_P_GPRE + 1, :] * (1.0 + sc1)) + sh1
    proj_s[...] = _dot(h.astype(BF16), w_in_ref[...])

    g0 = p512_ref[_P_GAMMA0:_P_GAMMA0 + 1, :]
    g1 = p512_ref[_P_GAMMA1:_P_GAMMA1 + 1, :]
    gm = jnp.maximum(g0, g1)
    e0 = jnp.exp(g0 - gm)
    lb = e0 / (e0 + jnp.exp(g1 - gm))
    hnorm = p512_ref[_P_HNORM:_P_HNORM + 1, :]

    for c in range(ts // CHUNK):
        r0 = c * CHUNK
        q = proj_s[r0:r0 + CHUNK, 0:dh]
        fz = proj_s[r0:r0 + CHUNK, dh:2 * dh]
        qs_s[...] = q * _sigmoid(q)
        ez = jnp.exp(-jnp.abs(fz))
        inv = 1.0 / (1.0 + ez)
        pos = fz >= 0.0
        sg = jnp.where(pos, inv, ez * inv)
        sn = jnp.where(pos, ez * inv, inv)
        f = lb + (1.0 - lb) * sg
        kk_s[...] = (1.0 - lb) * sn
        f_s[...] = f
        g = jnp.log(f) * LOG2E
        ghi = g.astype(BF16)
        glo = (g - ghi.astype(F32)).astype(BF16)
        b2 = _dot(hmat_ref[...], jnp.concatenate([ghi, glo], axis=1))
        ea_s[0:CHUNK, :] = b2[:, 0:dh] + b2[:, dh:2 * dh]
        for lv, blk in enumerate(_levels(CHUNK)):
            for g0 in range(0, CHUNK, 2 * blk):
                lv0 = (lv + 1) * CHUNK + g0
                ea_s[lv0:lv0 + 2 * blk, :] = ea_s[g0:g0 + 2 * blk, :] - ea_s[g0 + blk - 1:g0 + blk, :]

        for r in range(CHUNK // SUB):
            r8 = r * SUB
            q8 = qs_s[r8:r8 + SUB, :]
            k8 = kk_s[r8:r8 + SUB, :]
            f8 = f_s[r8:r8 + SUB, :]
            qd = q8
            tile = None
            for j in range(SUB):
                if j > 0:
                    qd = qd * (f8 if j == 1 else pltpu.roll(f8, j - 1, 0))
                a = qd * (k8 if j == 0 else pltpu.roll(k8, j, 0))
                m = dmask_ref[j, r8:r8 + SUB, :]
                parts = [jnp.sum(a[:, hd * HEAD_DIM:(hd + 1) * HEAD_DIM], axis=1, keepdims=True) * m
                         for hd in range(nh)]
                contrib = jnp.concatenate(parts, axis=1)
                tile = contrib if tile is None else tile + contrib
            sd_s[r8:r8 + SUB, :] = tile

        def bdiag(m):
            z = jnp.zeros((m.shape[0], HEAD_DIM), m.dtype)
            return jnp.concatenate([jnp.concatenate([m[:, :HEAD_DIM], z], axis=1),
                                    jnp.concatenate([z, m[:, HEAD_DIM:]], axis=1)], axis=0)

        pw = 2 * HEAD_DIM
        for pr in range(nh // 2):
            c0 = pr * pw
            qp = qs_s[:, c0:c0 + pw]
            kp = kk_s[:, c0:c0 + pw]
            vp = proj_s[r0:r0 + CHUNK, 2 * dh + c0:2 * dh + c0 + pw]
            b = ea_s[0:CHUNK, c0:c0 + pw]
            s_acc = sd_s[:, c0:c0 + pw]
            for lv in range(n_lv):
                e_lv = ea_s[(lv + 1) * CHUNK:(lv + 2) * CHUNK, c0:c0 + pw]
                w_lv = jnp.exp2(jnp.minimum(e_lv, -e_lv))
                s_acc = s_acc + hmask_ref[lv] * _dot_nt((qp * w_lv).astype(BF16), bdiag((kp * w_lv).astype(BF16)))
            o = _dot(s_acc.astype(BF16), bdiag(vp.astype(BF16)))
            st_pair = jnp.concatenate([st_s[2 * pr], st_s[2 * pr + 1]], axis=1)
            o = o + _dot_nt((qp * jnp.exp2(b)).astype(BF16), bdiag(st_pair.astype(BF16)))
            b_last = ea_s[CHUNK - 1:CHUNK, c0:c0 + pw]
            kd = (kp * jnp.exp2(b_last - b)).astype(BF16)
            new_st = st_pair * jnp.exp2(b_last)
            for h in range(2):
                l0 = h * HEAD_DIM
                st_s[2 * pr + h] = new_st[:, l0:l0 + HEAD_DIM] + _dot(
                    vp[:, l0:l0 + HEAD_DIM].T.astype(BF16), kd[:, l0:l0 + HEAD_DIM])
            od_s[:, c0:c0 + pw] = o

        gz = proj_s[r0:r0 + CHUNK, 3 * dh:4 * dh]
        sgz = gz * _sigmoid(gz)
        outs = []
        for hd in range(nh):
            c0 = hd * HEAD_DIM
            outs.append(_rms(od_s[:, c0:c0 + HEAD_DIM], hnorm[:, c0:c0 + HEAD_DIM]))
        o_s[r0:r0 + CHUNK, 0:dh] = (jnp.concatenate(outs, axis=1) * sgz).astype(BF16)

    xr = proj_s[:, 4 * dh:4 * dh + dr]
    yr = proj_s[:, 4 * dh + dr:4 * dh + 2 * dr]
    ext_s[SUBLANES:, :] = xr
    xc = p512_ref[_P_CW0 + CONV_WIDTH - 1:_P_CW0 + CONV_WIDTH, :] * xr + p512_ref[_P_CB:_P_CB + 1, :]
    for k in range(1, CONV_WIDTH):
        wk = p512_ref[_P_CW0 + CONV_WIDTH - 1 - k:_P_CW0 + CONV_WIDTH - k, :]
        xc = xc + wk * ext_s[SUBLANES - k:SUBLANES - k + ts, :]
    ext_s[0:SUBLANES, :] = xr[ts - SUBLANES:, :]

    xcb = xc.astype(BF16)
    half = dr // 2

    def gate(col0):
        return jnp.concatenate([_dot(xcb[:, :half], w_ri_ref[0:half, col0:col0 + half]),
                                _dot(xcb[:, half:], w_ri_ref[half:dr, col0 + half:col0 + dr])], axis=1)

    r_g = _sigmoid(gate(0) + p512_ref[_P_BR:_P_BR + 1, :])
    i_g = _sigmoid(gate(dr) + p512_ref[_P_BI:_P_BI + 1, :])
    lam = p512_ref[_P_LAM:_P_LAM + 1, :]
    log_sig = jnp.minimum(lam, 0.0) - jnp.log(1.0 + jnp.exp(-jnp.abs(lam)))
    a_t = jnp.exp(RG_C * r_g * log_sig)
    y_t = 1.0 - a_t * a_t
    u_t = y_t * lax.rsqrt(jnp.maximum(y_t, TINY)) * (i_g * xc)
    n_tiles = ts // SUBLANES
    a3 = a_t.reshape(n_tiles, SUBLANES, dr)
    u3 = u_t.reshape(n_tiles, SUBLANES, dr)
    row3 = lax.broadcasted_iota(jnp.int32, (n_tiles, SUBLANES, dr), 1)
    d = 1
    while d < SUBLANES:
        keep = row3 >= d
        a_prev = jnp.where(keep, pltpu.roll(a3, d, 1), 1.0)
        u_prev = jnp.where(keep, pltpu.roll(u3, d, 1), 0.0)
        u3 = a3 * u_prev + u3
        a3 = a3 * a_prev
        d *= 2
    carry = rgc_s[SUBLANES - 1:SUBLANES, :]
    h_tiles = []
    for t in range(n_tiles):
        h_t = a3[t] * carry + u3[t]
        h_tiles.append(h_t)
        carry = h_t[SUBLANES - 1:SUBLANES, :]
    h_rg = jnp.concatenate(h_tiles, axis=0)
    rgc_s[...] = h_tiles[-1]
    gelu = 0.5 * yr * (1.0 + jnp.tanh(0.7978845608028654 * (yr + 0.044715 * (yr * yr * yr))))
    o_s[:, dh:dh + dr] = _rms(h_rg * gelu, p512_ref[_P_RNORM:_P_RNORM + 1, :]).astype(BF16)

    mix = _dot(o_s[...], w_out_ref[...])
    x1 = x + gt1 * _rms(mix, p1024_ref[_P_GPOST:_P_GPOST + 1, :])
    x1_ref[0] = x1
    h2 = _rms(x1, p1024_ref[_P_GFFN:_P_GFFN + 1, :] * (1.0 + sc2)) + sh2
    _store_row_tiles(h2_ref.at[0], _pack_rows(h2))

    h_hi, h_mid, _ = _split3(h2)
    w_hi, w_mid, _ = _split3(w_rt_ref[...])
    lg = _dot(h_hi, w_hi) + (_dot(h_mid, w_hi) + _dot(h_hi, w_mid)) + b_rt_ref[...]
    lane = lax.broadcasted_iota(jnp.int32, (ts, LANES), 1)
    lane_f = lane.astype(F32)
    neg = jnp.float32(-jnp.inf)
    big = jnp.float32(2 * LANES)
    is_g = jnp.logical_and(lane >= GROUP_LANE0, lane < GROUP_LANE0 + N_GROUPS)
    gl = jnp.where(is_g, lg, neg)
    g_max = jnp.max(gl, axis=1, keepdims=True)
    g_star = jnp.min(jnp.where(gl == g_max, lane_f, big), axis=1, keepdims=True) - float(GROUP_LANE0)
    gate_g = 1.0 / jnp.sum(jnp.exp(gl - g_max), axis=1, keepdims=True)
    grp_f = jnp.floor(lane_f * (1.0 / EXPERTS_PER_GROUP))
    el = jnp.where(jnp.logical_and(lane < N_EXPERTS, grp_f == g_star), lg, neg)
    m1 = jnp.max(el, axis=1, keepdims=True)
    i1 = jnp.min(jnp.where(el == m1, lane_f, big), axis=1, keepdims=True)
    el2 = jnp.where(lane_f == i1, neg, el)
    m2 = jnp.max(el2, axis=1, keepdims=True)
    i2 = jnp.min(jnp.where(el2 == m2, lane_f, big), axis=1, keepdims=True)
    e21 = jnp.exp(m2 - m1)
    w0 = gate_g / (1.0 + e21)
    w1 = gate_g * e21 / (1.0 + e21)

    hit0 = lane_f == i1
    hit1 = lane_f == i2
    oh = jnp.where(jnp.logical_or(hit0, hit1), 1.0, 0.0)
    before = _dot(ltri_ref[...], oh.astype(BF16)) + cntc_s[0:1, :]
    rank0 = jnp.sum(jnp.where(hit0, before, 0.0), axis=1, keepdims=True)
    rank1 = jnp.sum(jnp.where(hit1, before, 0.0), axis=1, keepdims=True)
    cntc_s[...] = cntc_s[...] + jnp.sum(oh, axis=0, keepdims=True)
    cnt_ref[...] = cntc_s[...]

    route = jnp.where(lane == _R_E0, i1, 0.0)
    route = jnp.where(lane == _R_E1, i2, route)
    route = jnp.where(lane == _R_W0, w0, route)
    route = jnp.where(lane == _R_W1, w1, route)
    route = jnp.where(lane == _R_RANK0, rank0, route)
    route = jnp.where(lane == _R_RANK1, rank1, route)
    route_ref[...] = route


def _mix(x, mod, p1024, p512, w_in, w_ri, w_out, w_rt, b_rt):
    bsz, seq, d = x.shape
    ts = TILE_S
    dh = p512.shape[1]
    dr = p512.shape[1]
    nh = dh // HEAD_DIM
    n_s = seq // ts
    assert CHUNK == HEAD_DIM
    hmat_np, hmask_np, dmask_np = _hier_constants(CHUNK)
    dmask = jnp.asarray(dmask_np, F32)
    hmat = jnp.asarray(hmat_np, BF16)
    hmask = jnp.asarray(hmask_np, F32)
    tt = np.arange(ts)
    ltri = jnp.asarray((tt[None, :] < tt[:, None]).astype(np.float32), BF16)

    def const(shape):
        return pl.BlockSpec(shape, lambda b, s: (0,) * len(shape))

    return pl.pallas_call(
        _mix_kernel,
        grid=(bsz, n_s),
        in_specs=[
            pl.BlockSpec((1, ts, d), lambda b, s: (b, s, 0)),
            pl.BlockSpec((1, mod.shape[1], d), lambda b, s: (b, 0, 0)),
            const(p1024.shape), const(p512.shape), const(w_in.shape), const(w_ri.shape),
            const(w_out.shape), const(w_rt.shape), const(b_rt.shape),
            const(hmat.shape), const(hmask.shape), const(dmask.shape), const(ltri.shape),
        ],
        out_specs=[
            pl.BlockSpec((1, ts, d), lambda b, s: (b, s, 0)),
            pl.BlockSpec((1, ts * ROW_TILES, LANES), lambda b, s: (b, s, 0)),
            pl.BlockSpec((ts, LANES), lambda b, s: (b * n_s + s, 0)),
            pl.BlockSpec((SUBLANES, LANES), lambda b, s: (0, 0)),
        ],
        out_shape=[
            jax.ShapeDtypeStruct((bsz, seq, d), F32),
            jax.ShapeDtypeStruct((bsz, seq * ROW_TILES, LANES), jnp.uint32),
            jax.ShapeDtypeStruct((bsz * seq, LANES), F32),
            jax.ShapeDtypeStruct((SUBLANES, LANES), F32),
        ],
        scratch_shapes=[
            pltpu.VMEM((ts, w_in.shape[1]), F32),
            pltpu.VMEM((CHUNK, dh), F32),
            pltpu.VMEM((CHUNK, dh), F32),
            pltpu.VMEM((CHUNK, dh), F32),
            pltpu.VMEM((CHUNK, dh), F32),
            pltpu.VMEM(((1 + hmask.shape[0]) * CHUNK, dh), F32),
            pltpu.VMEM((CHUNK, dh), F32),
            pltpu.VMEM((ts, dh + dr), BF16),
            pltpu.VMEM((nh, HEAD_DIM, HEAD_DIM), F32),
            pltpu.VMEM((SUBLANES, dr), F32),
            pltpu.VMEM((ts + SUBLANES, dr), F32),
            pltpu.VMEM((SUBLANES, LANES), F32),
        ],
        compiler_params=pltpu.CompilerParams(
            dimension_semantics=("arbitrary", "arbitrary"), vmem_limit_bytes=VMEM_LIMIT_BYTES),
        name="mix_route",
    )(x, mod, p1024, p512, w_in, w_ri, w_out, w_rt, b_rt, hmat, hmask, dmask, ltri)


def _dispatch_kernel(tail_ref, dest_ref, h2_ref, xb_ref, zero_s, sem, zsem):
    tt = h2_ref.shape[0]

    @pl.when(pl.program_id(0) == 0)
    def _():
        zero_s[...] = jnp.zeros_like(zero_s)

        def tail_copy(e):
            row0 = pl.multiple_of(jnp.maximum(tail_ref[e], 0), MOE_BLOCK)
            return pltpu.make_async_copy(zero_s, xb_ref.at[pl.ds(row0, MOE_BLOCK)], zsem)

        for e in range(tail_ref.shape[0]):
            @pl.when(tail_ref[e] >= 0)
            def _():
                tail_copy(e).start()
        for e in range(tail_ref.shape[0]):
            @pl.when(tail_ref[e] >= 0)
            def _():
                tail_copy(e).wait()

    def row_copy(t, slot):
        return pltpu.make_async_copy(h2_ref.at[pl.ds(t, 1)], xb_ref.at[pl.ds(slot, 1)], sem)

    def issue(t, carry):
        row_copy(t, dest_ref[0, 0, 2 * t]).start()
        row_copy(t, dest_ref[0, 0, 2 * t + 1]).start(priority=1)
        return carry

    lax.fori_loop(0, tt, issue, 0, unroll=8)

    def drain(t, carry):
        row_copy(0, 0).wait()
        row_copy(0, 0).wait()
        return carry

    lax.fori_loop(0, tt, drain, 0, unroll=8)


def _dispatch(tails, dest, h2, n_slots):
    n_tok = h2.shape[0]
    row = h2.shape[1:]
    tt = dest.shape[2] // TOP_K
    grid_spec = pltpu.PrefetchScalarGridSpec(
        num_scalar_prefetch=1,
        grid=(n_tok // tt,),
        in_specs=[
            pl.BlockSpec((1, 1, TOP_K * tt), lambda i, tl: (i, 0, 0), memory_space=pltpu.SMEM),
            pl.BlockSpec((tt,) + row, lambda i, tl: (i, 0, 0)),
        ],
        out_specs=pl.BlockSpec(memory_space=pl.ANY),
        scratch_shapes=[pltpu.VMEM((MOE_BLOCK,) + row, h2.dtype),
                        pltpu.SemaphoreType.DMA(()), pltpu.SemaphoreType.DMA(())],
    )
    return pl.pallas_call(
        _dispatch_kernel,
        grid_spec=grid_spec,
        out_shape=jax.ShapeDtypeStruct((n_slots,) + row, h2.dtype),
        compiler_params=pltpu.CompilerParams(dimension_semantics=("arbitrary",)),
        name="dispatch_rows",
    )(tails, dest, h2)


def _ffn_kernel(first_ref, nblk_ref, spare_ref, w1_ref, w3_ref, w2_ref, xb_ref, yb_ref,
                w1b, w3b, w2b, xbuf, ybuf, xt2, yt2, xt1, yt1, xsem, ysem, tsem, zsem):
    e = pl.program_id(0)
    blk_rows = xt1.shape[0]
    grp_rows = xbuf.shape[1]
    nb = nblk_ref[e]
    n_grp = nb // FFN_GROUP
    n_tail = nb - n_grp * FFN_GROUP
    row0 = first_ref[e] * blk_rows
    tail0 = row0 + n_grp * grp_rows

    def rows_at(start, n):
        return pl.ds(pl.multiple_of(start, blk_rows), n)

    def x_copy(i, slot):
        return pltpu.make_async_copy(xb_ref.at[rows_at(row0 + i * grp_rows, grp_rows)], xbuf.at[slot], xsem.at[slot])

    def y_copy(i, slot):
        return pltpu.make_async_copy(ybuf.at[slot], yb_ref.at[rows_at(row0 + i * grp_rows, grp_rows)], ysem.at[slot])

    def tail_piece(p, off_blocks):
        x_t, y_t = tails[p]
        hbm_rows = rows_at(tail0 + off_blocks * blk_rows, x_t.shape[0])
        return (pltpu.make_async_copy(xb_ref.at[hbm_rows], x_t, tsem.at[p, 0]),
                pltpu.make_async_copy(y_t, yb_ref.at[hbm_rows], tsem.at[p, 1]))

    def expert_rows(x_view, y_view):
        xv = _unpack_rows(_load_row_tiles(x_view)).astype(BF16)
        a = _dot(xv, w1b[...])
        g = _dot(xv, w3b[...])
        hact = (a * _sigmoid(a) * g).astype(BF16)
        _store_row_tiles(y_view, _pack_rows(_dot(hact, w2b[...])))

    @pl.when(e == 0)
    def _():
        yt1[...] = jnp.zeros_like(yt1)

        def spare_copy(j):
            r = pl.multiple_of(jnp.maximum(spare_ref[j], 0) * blk_rows, blk_rows)
            return pltpu.make_async_copy(yt1, yb_ref.at[pl.ds(r, blk_rows)], zsem)

        for j in range(spare_ref.shape[0]):
            @pl.when(spare_ref[j] >= 0)
            def _():
                spare_copy(j).start()
        for j in range(spare_ref.shape[0]):
            @pl.when(spare_ref[j] >= 0)
            def _():
                spare_copy(j).wait()

    tails = ((xt2, yt2), (xt1, yt1))
    pieces = ((0, 0, n_tail >= 2), (1, n_tail - n_tail % 2, n_tail % 2 == 1))

    @pl.when(nb > 0)
    def _():
        @pl.when(n_grp > 0)
        def _():
            x_copy(0, 0).start()

        for p, off, present in pieces:
            @pl.when(present)
            def _():
                tail_piece(p, off)[0].start()

        w1b[...] = w1_ref[0].astype(BF16)
        w3b[...] = w3_ref[0].astype(BF16)
        w2b[...] = w2_ref[0].astype(BF16)

        def body(i, carry):
            slot = i % 2
            x_copy(i, slot).wait()

            @pl.when(i + 1 < n_grp)
            def _():
                x_copy(i + 1, 1 - slot).start()

            @pl.when(i >= 2)
            def _():
                y_copy(i - 2, slot).wait()

            expert_rows(xbuf.at[slot], ybuf.at[slot])
            y_copy(i, slot).start()
            return carry

        lax.fori_loop(0, n_grp, body, 0)

        for p, off, present in pieces:
            @pl.when(present)
            def _():
                x_cp, y_cp = tail_piece(p, off)
                x_cp.wait()
                expert_rows(*tails[p])
                y_cp.start()

        @pl.when(n_grp >= 2)
        def _():
            y_copy(n_grp - 2, n_grp % 2).wait()

        @pl.when(n_grp >= 1)
        def _():
            y_copy(n_grp - 1, (n_grp - 1) % 2).wait()

        for p, off, present in pieces:
            @pl.when(present)
            def _():
                tail_piece(p, off)[1].wait()


def _ffn(first_blk, n_blk, spare_blk, xb, w1, w3, w2):
    n_rows, lanes = xb.shape
    blk_rows = MOE_BLOCK * ROW_TILES
    n_exp, d, de = w1.shape
    assert FFN_GROUP == 4

    def rows_buf(*lead):
        return pltpu.VMEM(lead + (lanes,), xb.dtype)

    def w_map(e, *_):
        return (e, 0, 0)

    grid_spec = pltpu.PrefetchScalarGridSpec(
        num_scalar_prefetch=3,
        grid=(n_exp,),
        in_specs=[
            pl.BlockSpec((1, d, de), w_map),
            pl.BlockSpec((1, d, de), w_map),
            pl.BlockSpec((1, de, d), w_map),
            pl.BlockSpec(memory_space=pl.ANY),
        ],
        out_specs=pl.BlockSpec(memory_space=pl.ANY),
        scratch_shapes=[pltpu.VMEM((d, de), BF16), pltpu.VMEM((d, de), BF16), pltpu.VMEM((de, d), BF16),
                        rows_buf(2, FFN_GROUP * blk_rows), rows_buf(2, FFN_GROUP * blk_rows),
                        rows_buf(2 * blk_rows), rows_buf(2 * blk_rows), rows_buf(blk_rows), rows_buf(blk_rows),
                        pltpu.SemaphoreType.DMA((2,)), pltpu.SemaphoreType.DMA((2,)),
                        pltpu.SemaphoreType.DMA((2, 2)), pltpu.SemaphoreType.DMA(())],
    )
    return pl.pallas_call(
        _ffn_kernel,
        grid_spec=grid_spec,
        out_shape=jax.ShapeDtypeStruct((n_rows, lanes), xb.dtype),
        compiler_params=pltpu.CompilerParams(
            dimension_semantics=("arbitrary",), vmem_limit_bytes=VMEM_LIMIT_BYTES),
        name="expert_ffn",
    )(first_blk, n_blk, spare_blk, w1, w3, w2, xb)


def _combine_kernel(dest_ref, dest_next_ref, yb_ref, route_ref, x1_ref, mod_ref, g_ref, out_ref, ybuf, sem):
    tt = x1_ref.shape[0]
    i = pl.program_id(0)
    cur = i % 2

    def row_copy(slot, buf, k, t):
        return pltpu.make_async_copy(yb_ref.at[slot], ybuf.at[buf, k, pl.ds(t * ROW_TILES, ROW_TILES)], sem.at[buf])

    def issue(d_ref, buf):
        def body(t, carry):
            row_copy(d_ref[0, 0, 2 * t], buf, 0, t).start()
            row_copy(d_ref[0, 0, 2 * t + 1], buf, 1, t).start(priority=1)
            return carry
        lax.fori_loop(0, tt, body, 0, unroll=8)

    @pl.when(i == 0)
    def _():
        issue(dest_ref, 0)

    @pl.when(i + 1 < pl.num_programs(0))
    def _():
        issue(dest_next_ref, 1 - cur)

    def drain(t, carry):
        row_copy(0, cur, 0, 0).wait()
        row_copy(0, cur, 0, 0).wait()
        return carry

    lax.fori_loop(0, tt, drain, 0, unroll=8)

    w0 = route_ref[:, _R_W0:_R_W0 + 1]
    w1 = route_ref[:, _R_W1:_R_W1 + 1]
    y = (_unpack_rows(_load_row_tiles(ybuf.at[cur, 0])) * w0
         + _unpack_rows(_load_row_tiles(ybuf.at[cur, 1])) * w1)
    gt2 = mod_ref[0, 5:6, :]
    out_ref[...] = x1_ref[...] + gt2 * _rms(y, g_ref[...])


def _combine(dest, yb, route, x1, mod, g_post_ffn, seq):
    n_tok, d = x1.shape
    tt = dest.shape[2] // TOP_K
    n_steps = n_tok // tt
    per_b = seq // tt
    return pl.pallas_call(
        _combine_kernel,
        grid=(n_steps,),
        in_specs=[
            pl.BlockSpec((1, 1, TOP_K * tt), lambda i: (i, 0, 0), memory_space=pltpu.SMEM),
            pl.BlockSpec((1, 1, TOP_K * tt), lambda i: (jnp.minimum(i + 1, n_steps - 1), 0, 0),
                         memory_space=pltpu.SMEM),
            pl.BlockSpec(memory_space=pl.ANY),
            pl.BlockSpec((tt, LANES), lambda i: (i, 0)),
            pl.BlockSpec((tt, d), lambda i: (i, 0)),
            pl.BlockSpec((1, mod.shape[1], d), lambda i: (i // per_b, 0, 0)),
            pl.BlockSpec((1, d), lambda i: (0, 0)),
        ],
        out_specs=pl.BlockSpec((tt, d), lambda i: (i, 0)),
        out_shape=jax.ShapeDtypeStruct((n_tok, d), F32),
        scratch_shapes=[pltpu.VMEM((2, TOP_K, tt * ROW_TILES, LANES), yb.dtype), pltpu.SemaphoreType.DMA((2,))],
        compiler_params=pltpu.CompilerParams(
            dimension_semantics=("arbitrary",), vmem_limit_bytes=VMEM_LIMIT_BYTES),
        name="combine_rows",
    )(dest, dest, yb, route, x1, mod, g_post_ffn)


def _block_diag(w):
    nb, bi, bo = w.shape
    eye = jnp.eye(nb, dtype=w.dtype)
    return (w[:, :, None, :] * eye[:, None, :, None]).reshape(nb * bi, nb * bo)


def kernel(x, c, w_ada, b_ada, g_pre_mix, g_post_mix, g_pre_ffn, g_post_ffn, w_in, hgrn_gamma, hgrn_norm_g, conv_w, conv_b, rg_w_r, rg_b_r, rg_w_i, rg_b_i, rg_lambda, rg_norm_g, w_out, w_router_group, b_router_group, w_router_expert, b_router_expert, w1, w3, w2):
    bsz, seq, d = x.shape
    depth = w_ada.shape[0]
    assert depth == 1 and hgrn_gamma.shape[0] == 2
    dh = hgrn_norm_g.shape[1]
    dr = rg_norm_g.shape[1]
    assert dh == dr and dh % (2 * HEAD_DIM) == 0 and seq % TILE_S == 0 and TILE_S % CHUNK == 0
    assert (dr // 2) % (dr // rg_w_r.shape[1]) == 0
    assert w_router_expert.shape[2] == N_EXPERTS and w_router_group.shape[2] == N_GROUPS
    n_tok = bsz * seq

    c8 = jnp.pad(c, ((0, SUBLANES - bsz), (0, 0)))
    p512 = jnp.concatenate([
        hgrn_gamma, hgrn_norm_g, conv_w[0], conv_b, rg_b_r, rg_b_i, rg_lambda, rg_norm_g,
        jnp.zeros((4, dh), F32)], axis=0)
    p1024 = jnp.concatenate([g_pre_mix, g_post_mix, g_pre_ffn, jnp.zeros((5, d), F32)], axis=0)
    w_ri = jnp.concatenate([_block_diag(rg_w_r[0]), _block_diag(rg_w_i[0])], axis=1).astype(BF16)
    w_rt = jnp.concatenate([w_router_expert[0], w_router_group[0],
                            jnp.zeros((d, LANES - N_EXPERTS - N_GROUPS), F32)], axis=1)
    b_rt = jnp.concatenate([b_router_expert[0], b_router_group[0],
                            jnp.zeros((LANES - N_EXPERTS - N_GROUPS,), F32)])[None, :]

    mod = _ada(c8, w_ada[0], b_ada[0][None, :])[:bsz].reshape(bsz, 6, d)

    x1, h2, route, cnt = _mix(x, mod, p1024, p512, w_in[0].astype(BF16), w_ri, w_out[0].astype(BF16), w_rt, b_rt)

    counts = cnt[0, :N_EXPERTS].astype(jnp.int32)
    pcounts = (counts + MOE_BLOCK - 1) // MOE_BLOCK * MOE_BLOCK
    pends = jnp.cumsum(pcounts)
    pstarts = pends - pcounts
    n_blocks = -(-(n_tok * TOP_K) // MOE_BLOCK) + N_EXPERTS
    eids = route[:, _R_E0:_R_E1 + 1].astype(jnp.int32)
    ranks = route[:, _R_RANK0:_R_RANK1 + 1].astype(jnp.int32)
    is_e = eids[:, :, None] == jnp.arange(N_EXPERTS, dtype=jnp.int32)
    dest = jnp.sum(jnp.where(is_e, pstarts, 0), axis=2) + ranks
    spare = pends[-1] + jnp.arange(N_EXPERTS, dtype=jnp.int32) * MOE_BLOCK
    tails = jnp.concatenate([jnp.where(pcounts > 0, pends - MOE_BLOCK, -1),
                             jnp.where(spare < n_blocks * MOE_BLOCK, spare, -1)]).astype(jnp.int32)

    n_slots = n_blocks * MOE_BLOCK
    dest_tiles = dest.reshape(n_tok // TILE_T, 1, TOP_K * TILE_T)
    xb = _dispatch(tails, dest_tiles, h2.reshape(n_tok, ROW_TILES, LANES), n_slots)
    spare_blk = jnp.where(spare < n_slots, spare // MOE_BLOCK, -1).astype(jnp.int32)
    yb = _ffn((pstarts // MOE_BLOCK).astype(jnp.int32), (pcounts // MOE_BLOCK).astype(jnp.int32), spare_blk,
              xb.reshape(n_slots * ROW_TILES, LANES), w1[0], w3[0], w2[0])
    out = _combine(dest_tiles, yb.reshape(n_slots, ROW_TILES, LANES), route,
                   x1.reshape(n_tok, d), mod, g_post_ffn, seq)
    return out.reshape(bsz, seq, d)
```

```python
import functools

import numpy as np
import jax
import jax.numpy as jnp
from jax import lax
from jax.experimental import pallas as pl
from jax.experimental.pallas import tpu as pltpu

F32 = jnp.float32
BF16 = jnp.bfloat16

LANES = 128
SUBLANES = 8
ROW_TILES = 4
VMEM_LIMIT_BYTES = 56 * 1024 * 1024

EPS = 1e-6
HEAD_DIM = 128
CHUNK = 128
SUB = 8
RG_C = 8.0
LOG2E = 1.4426950408889634
TINY = 1e-30
RG_BLOCKS = 8
CONV_WIDTH = 4
N_GROUPS = 4
EXPERTS_PER_GROUP = 8
N_EXPERTS = N_GROUPS * EXPERTS_PER_GROUP
TOP_K = 2
MOE_BLOCK = 256
FFN_GROUP = 4
TILE_S = 512
TILE_T = 1024
COMBINE_ROWS = 128
GROUP_LANE0 = N_EXPERTS


def _levels(chunk):
    out, b = [], chunk // 2
    while b >= SUB:
        out.append(b)
        b //= 2
    return out


def _hier_constants(chunk):
    t = np.arange(chunk)
    low = (t[None, :] <= t[:, None]).astype(np.float32)
    masks = []
    for b in _levels(chunk):
        same = (t[:, None] // (2 * b)) == (t[None, :] // (2 * b))
        upper = (t[:, None] % (2 * b)) >= b
        lower = (t[None, :] % (2 * b)) < b
        masks.append(np.tile((same & upper & lower).astype(np.float32), (1, 2)))
    direct = [((t[None, :] == t[:, None] - j) & ((t[:, None] % SUB) >= j)).astype(np.float32) for j in range(SUB)]
    return low, np.stack(masks, 0), np.stack(direct, 0)


def _rms(v, g):
    return v * lax.rsqrt(jnp.mean(v * v, axis=-1, keepdims=True) + EPS) * g


def _sigmoid(v):
    return 0.5 * jnp.tanh(0.5 * v) + 0.5


def _dot(a, b):
    return jnp.dot(a, b, preferred_element_type=F32)


def _dot_nt(a, b):
    return lax.dot_general(a, b, (((1,), (1,)), ((), ())), preferred_element_type=F32)


def _pack_rows(v):
    half = v.shape[1] // 2
    vb = v.astype(BF16).astype(F32)
    hi = lax.bitcast_convert_type(vb[:, :half], jnp.uint32)
    lo = lax.bitcast_convert_type(vb[:, half:], jnp.uint32)
    return hi | lax.shift_right_logical(lo, jnp.uint32(16))


def _unpack_rows(w):
    hi = lax.bitcast_convert_type(w & jnp.uint32(0xFFFF0000), F32)
    lo = lax.bitcast_convert_type(lax.shift_left(w, jnp.uint32(16)), F32)
    return jnp.concatenate([hi, lo], axis=1)


def _store_row_tiles(ref, w):
    n = w.shape[0]
    for j in range(ROW_TILES):
        ref[pl.ds(j, n, stride=ROW_TILES), :] = w[:, j * LANES:(j + 1) * LANES]


def _load_row_tiles(ref):
    n = ref.shape[0] // ROW_TILES
    return jnp.concatenate([ref[pl.ds(j, n, stride=ROW_TILES), :] for j in range(ROW_TILES)], axis=1)


def _split3(v):
    hi = v.astype(BF16)
    r1 = v - hi.astype(F32)
    mid = r1.astype(BF16)
    lo = (r1 - mid.astype(F32)).astype(BF16)
    return hi, mid, lo


def _ada_kernel(c_ref, w_ref, b_ref, o_ref):
    c = c_ref[...]
    sc = c * _sigmoid(c)
    s_hi, s_mid, _ = _split3(sc)
    w = w_ref[...]
    w_hi, w_mid, _ = _split3(w)
    acc = _dot(s_hi, w_hi) + (_dot(s_mid, w_hi) + _dot(s_hi, w_mid))
    o_ref[...] = acc + b_ref[...]


def _ada(c8, w_ada, b_ada):
    d, n = w_ada.shape
    tn = 512
    return pl.pallas_call(
        _ada_kernel,
        grid=(n // tn,),
        in_specs=[pl.BlockSpec((SUBLANES, d), lambda j: (0, 0)),
                  pl.BlockSpec((d, tn), lambda j: (0, j)),
                  pl.BlockSpec((1, tn), lambda j: (0, j))],
        out_specs=pl.BlockSpec((SUBLANES, tn), lambda j: (0, j)),
        out_shape=jax.ShapeDtypeStruct((SUBLANES, n), F32),
        compiler_params=pltpu.CompilerParams(dimension_semantics=("arbitrary",)),
        name="ada_mod",
    )(c8, w_ada, b_ada)


_P_GAMMA0, _P_GAMMA1, _P_HNORM, _P_CW0, _P_CB, _P_BR, _P_BI, _P_LAM, _P_RNORM = 0, 1, 2, 3, 7, 8, 9, 10, 11
_P_GPRE, _P_GPOST, _P_GFFN = 0, 1, 2
_R_E0, _R_E1, _R_W0, _R_W1, _R_RANK0, _R_RANK1 = 0, 1, 2, 3, 4, 5


def _mix_kernel(x_ref, mod_ref, p1024_ref, p512_ref, w_in_ref, w_ri_ref, w_out_ref, w_rt_ref, b_rt_ref,
                hmat_ref, hmask_ref, dmask_ref, ltri_ref,
                x1_ref, h2_ref, route_ref, cnt_ref,
                proj_s, qs_s, kk_s, f_s, sd_s, ea_s, od_s, o_s, st_s, rgc_s, ext_s, cntc_s):
    ts = x_ref.shape[1]
    dh = qs_s.shape[1]
    dr = rgc_s.shape[1]
    nh = dh // HEAD_DIM
    n_lv = hmask_ref.shape[0]
    si = pl.program_id(1)

    @pl.when(si == 0)
    def _():
        st_s[...] = jnp.zeros_like(st_s)
        rgc_s[...] = jnp.zeros_like(rgc_s)
        ext_s[0:SUBLANES, :] = jnp.zeros((SUBLANES, dr), F32)

    @pl.when(jnp.logical_and(pl.program_id(0) == 0, si == 0))
    def _():
        cntc_s[...] = jnp.zeros_like(cntc_s)

    x = x_ref[0]
    sh1, sc1, gt1 = mod_ref[0, 0:1, :], mod_ref[0, 1:2, :], mod_ref[0, 2:3, :]
    sh2, sc2 = mod_ref[0, 3:4, :], mod_ref[0, 4:5, :]

    h = _rms(x, p1024_ref[_P_GPRE:_P_GPRE + 1, :] * (1.0 + sc1)) + sh1
    proj_s[...] = _dot(h.astype(BF16), w_in_ref[...])

    g0 = p512_ref[_P_GAMMA0:_P_GAMMA0 + 1, :]
    g1 = p512_ref[_P_GAMMA1:_P_GAMMA1 + 1, :]
    gm = jnp.maximum(g0, g1)
    e0 = jnp.exp(g0 - gm)
    lb = e0 / (e0 + jnp.exp(g1 - gm))
    hnorm = p512_ref[_P_HNORM:_P_HNORM + 1, :]

    for c in range(ts // CHUNK):
        r0 = c * CHUNK
        q = proj_s[r0:r0 + CHUNK, 0:dh]
        fz = proj_s[r0:r0 + CHUNK, dh:2 * dh]
        qs_s[...] = q * _sigmoid(q)
        ez = jnp.exp(-jnp.abs(fz))
        inv = 1.0 / (1.0 + ez)
        pos = fz >= 0.0
        sg = jnp.where(pos, inv, ez * inv)
        sn = jnp.where(pos, ez * inv, inv)
        f = lb + (1.0 - lb) * sg
        kk_s[...] = (1.0 - lb) * sn
        f_s[...] = f
        g = jnp.log(f) * LOG2E
        ghi = g.astype(BF16)
        glo = (g - ghi.astype(F32)).astype(BF16)
        b2 = _dot(hmat_ref[...], jnp.concatenate([ghi, glo], axis=1))
        ea_s[0:CHUNK, :] = b2[:, 0:dh] + b2[:, dh:2 * dh]
        for lv, blk in enumerate(_levels(CHUNK)):
            for g0 in range(0, CHUNK, 2 * blk):
                lv0 = (lv + 1) * CHUNK + g0
                ea_s[lv0:lv0 + 2 * blk, :] = ea_s[g0:g0 + 2 * blk, :] - ea_s[g0 + blk - 1:g0 + blk, :]

        for r in range(CHUNK // SUB):
            r8 = r * SUB
            q8 = qs_s[r8:r8 + SUB, :]
            k8 = kk_s[r8:r8 + SUB, :]
            f8 = f_s[r8:r8 + SUB, :]
            qd = q8
            tile = None
            for j in range(SUB):
                if j > 0:
                    qd = qd * (f8 if j == 1 else pltpu.roll(f8, j - 1, 0))
                a = qd * (k8 if j == 0 else pltpu.roll(k8, j, 0))
                m = dmask_ref[j, r8:r8 + SUB, :]
                parts = [jnp.sum(a[:, hd * HEAD_DIM:(hd + 1) * HEAD_DIM], axis=1, keepdims=True) * m
                         for hd in range(nh)]
                contrib = jnp.concatenate(parts, axis=1)
                tile = contrib if tile is None else tile + contrib
            sd_s[r8:r8 + SUB, :] = tile

        def bdiag(m):
            z = jnp.zeros((m.shape[0], HEAD_DIM), m.dtype)
            return jnp.concatenate([jnp.concatenate([m[:, :HEAD_DIM], z], axis=1),
                                    jnp.concatenate([z, m[:, HEAD_DIM:]], axis=1)], axis=0)

        pw = 2 * HEAD_DIM
        for pr in range(nh // 2):
            c0 = pr * pw
            qp = qs_s[:, c0:c0 + pw]
            kp = kk_s[:, c0:c0 + pw]
            vp = proj_s[r0:r0 + CHUNK, 2 * dh + c0:2 * dh + c0 + pw]
            b = ea_s[0:CHUNK, c0:c0 + pw]
            s_acc = sd_s[:, c0:c0 + pw]
            for lv in range(n_lv):
                e_lv = ea_s[(lv + 1) * CHUNK:(lv + 2) * CHUNK, c0:c0 + pw]
                w_lv = jnp.exp2(jnp.minimum(e_lv, -e_lv))
                s_acc = s_acc + hmask_ref[lv] * _dot_nt((qp * w_lv).astype(BF16), bdiag((kp * w_lv).astype(BF16)))
            o = _dot(s_acc.astype(BF16), bdiag(vp.astype(BF16)))
            st_pair = jnp.concatenate([st_s[2 * pr], st_s[2 * pr + 1]], axis=1)
            o = o + _dot_nt((qp * jnp.exp2(b)).astype(BF16), bdiag(st_pair.astype(BF16)))
            b_last = ea_s[CHUNK - 1:CHUNK, c0:c0 + pw]
            kd = (kp * jnp.exp2(b_last - b)).astype(BF16)
            new_st = st_pair * jnp.exp2(b_last)
            for h in range(2):
                l0 = h * HEAD_DIM
                st_s[2 * pr + h] = new_st[:, l0:l0 + HEAD_DIM] + _dot(
                    vp[:, l0:l0 + HEAD_DIM].T.astype(BF16), kd[:, l0:l0 + HEAD_DIM])
            od_s[:, c0:c0 + pw] = o

        gz = proj_s[r0:r0 + CHUNK, 3 * dh:4 * dh]
        sgz = gz * _sigmoid(gz)
        outs = []
        for hd in range(nh):
            c0 = hd * HEAD_DIM
            outs.append(_rms(od_s[:, c0:c0 + HEAD_DIM], hnorm[:, c0:c0 + HEAD_DIM]))
        o_s[r0:r0 + CHUNK, 0:dh] = (jnp.concatenate(outs, axis=1) * sgz).astype(BF16)

    xr = proj_s[:, 4 * dh:4 * dh + dr]
    yr = proj_s[:, 4 * dh + dr:4 * dh + 2 * dr]
    ext_s[SUBLANES:, :] = xr
    xc = p512_ref[_P_CW0 + CONV_WIDTH - 1:_P_CW0 + CONV_WIDTH, :] * xr + p512_ref[_P_CB:_P_CB + 1, :]
    for k in range(1, CONV_WIDTH):
        wk = p512_ref[_P_CW0 + CONV_WIDTH - 1 - k:_P_CW0 + CONV_WIDTH - k, :]
        xc = xc + wk * ext_s[SUBLANES - k:SUBLANES - k + ts, :]
    ext_s[0:SUBLANES, :] = xr[ts - SUBLANES:, :]

    xcb = xc.astype(BF16)
    half = dr // 2

    def gate(col0):
        return jnp.concatenate([_dot(xcb[:, :half], w_ri_ref[0:half, col0:col0 + half]),
                                _dot(xcb[:, half:], w_ri_ref[half:dr, col0 + half:col0 + dr])], axis=1)

    r_g = _sigmoid(gate(0) + p512_ref[_P_BR:_P_BR + 1, :])
    i_g = _sigmoid(gate(dr) + p512_ref[_P_BI:_P_BI + 1, :])
    lam = p512_ref[_P_LAM:_P_LAM + 1, :]
    log_sig = jnp.minimum(lam, 0.0) - jnp.log(1.0 + jnp.exp(-jnp.abs(lam)))
    a_t = jnp.exp(RG_C * r_g * log_sig)
    y_t = 1.0 - a_t * a_t
    u_t = y_t * lax.rsqrt(jnp.maximum(y_t, TINY)) * (i_g * xc)
    n_tiles = ts // SUBLANES
    a3 = a_t.reshape(n_tiles, SUBLANES, dr)
    u3 = u_t.reshape(n_tiles, SUBLANES, dr)
    row3 = lax.broadcasted_iota(jnp.int32, (n_tiles, SUBLANES, dr), 1)
    d = 1
    while d < SUBLANES:
        keep = row3 >= d
        a_prev = jnp.where(keep, pltpu.roll(a3, d, 1), 1.0)
        u_prev = jnp.where(keep, pltpu.roll(u3, d, 1), 0.0)
        u3 = a3 * u_prev + u3
        a3 = a3 * a_prev
        d *= 2
    carry = rgc_s[SUBLANES - 1:SUBLANES, :]
    h_tiles = []
    for t in range(n_tiles):
        h_t = a3[t] * carry + u3[t]
        h_tiles.append(h_t)
        carry = h_t[SUBLANES - 1:SUBLANES, :]
    h_rg = jnp.concatenate(h_tiles, axis=0)
    rgc_s[...] = h_tiles[-1]
    gelu = 0.5 * yr * (1.0 + jnp.tanh(0.7978845608028654 * (yr + 0.044715 * (yr * yr * yr))))
    o_s[:, dh:dh + dr] = _rms(h_rg * gelu, p512_ref[_P_RNORM:_P_RNORM + 1, :]).astype(BF16)

    mix = _dot(o_s[...], w_out_ref[...])
    x1 = x + gt1 * _rms(mix, p1024_ref[_P_GPOST:_P_GPOST + 1, :])
    x1_ref[0] = x1
    h2 = _rms(x1, p1024_ref[_P_GFFN:_P_GFFN + 1, :] * (1.0 + sc2)) + sh2
    _store_row_tiles(h2_ref.at[0], _pack_rows(h2))

    h_hi, h_mid, _ = _split3(h2)
    w_hi, w_mid, _ = _split3(w_rt_ref[...])
    lg = _dot(h_hi, w_hi) + (_dot(h_mid, w_hi) + _dot(h_hi, w_mid)) + b_rt_ref[...]
    lane = lax.broadcasted_iota(jnp.int32, (ts, LANES), 1)
    lane_f = lane.astype(F32)
    neg = jnp.float32(-jnp.inf)
    big = jnp.float32(2 * LANES)
    is_g = jnp.logical_and(lane >= GROUP_LANE0, lane < GROUP_LANE0 + N_GROUPS)
    gl = jnp.where(is_g, lg, neg)
    g_max = jnp.max(gl, axis=1, keepdims=True)
    g_star = jnp.min(jnp.where(gl == g_max, lane_f, big), axis=1, keepdims=True) - float(GROUP_LANE0)
    gate_g = 1.0 / jnp.sum(jnp.exp(gl - g_max), axis=1, keepdims=True)
    grp_f = jnp.floor(lane_f * (1.0 / EXPERTS_PER_GROUP))
    el = jnp.where(jnp.logical_and(lane < N_EXPERTS, grp_f == g_star), lg, neg)
    m1 = jnp.max(el, axis=1, keepdims=True)
    i1 = jnp.min(jnp.where(el == m1, lane_f, big), axis=1, keepdims=True)
    el2 = jnp.where(lane_f == i1, neg, el)
    m2 = jnp.max(el2, axis=1, keepdims=True)
    i2 = jnp.min(jnp.where(el2 == m2, lane_f, big), axis=1, keepdims=True)
    e21 = jnp.exp(m2 - m1)
    w0 = gate_g / (1.0 + e21)
    w1 = gate_g * e21 / (1.0 + e21)

    hit0 = lane_f == i1
    hit1 = lane_f == i2
    oh = jnp.where(jnp.logical_or(hit0, hit1), 1.0, 0.0)
    before = _dot(ltri_ref[...], oh.astype(BF16)) + cntc_s[0:1, :]
    rank0 = jnp.sum(jnp.where(hit0, before, 0.0), axis=1, keepdims=True)
    rank1 = jnp.sum(jnp.where(hit1, before, 0.0), axis=1, keepdims=True)
    cntc_s[...] = cntc_s[...] + jnp.sum(oh, axis=0, keepdims=True)
    cnt_ref[...] = cntc_s[...]

    route = jnp.where(lane == _R_E0, i1, 0.0)
    route = jnp.where(lane == _R_E1, i2, route)
    route = jnp.where(lane == _R_W0, w0, route)
    route = jnp.where(lane == _R_W1, w1, route)
    route = jnp.where(lane == _R_RANK0, rank0, route)
    route = jnp.where(lane == _R_RANK1, rank1, route)
    route_ref[...] = route


def _mix(x, mod, p1024, p512, w_in, w_ri, w_out, w_rt, b_rt):
    bsz, seq, d = x.shape
    ts = TILE_S
    dh = p512.shape[1]
    dr = p512.shape[1]
    nh = dh // HEAD_DIM
    n_s = seq // ts
    assert CHUNK == HEAD_DIM
    hmat_np, hmask_np, dmask_np = _hier_constants(CHUNK)
    dmask = jnp.asarray(dmask_np, F32)
    hmat = jnp.asarray(hmat_np, BF16)
    hmask = jnp.asarray(hmask_np, F32)
    tt = np.arange(ts)
    ltri = jnp.asarray((tt[None, :] < tt[:, None]).astype(np.float32), BF16)

    def const(shape):
        return pl.BlockSpec(shape, lambda b, s: (0,) * len(shape))

    return pl.pallas_call(
        _mix_kernel,
        grid=(bsz, n_s),
        in_specs=[
            pl.BlockSpec((1, ts, d), lambda b, s: (b, s, 0)),
            pl.BlockSpec((1, mod.shape[1], d), lambda b, s: (b, 0, 0)),
            const(p1024.shape), const(p512.shape), const(w_in.shape), const(w_ri.shape),
            const(w_out.shape), const(w_rt.shape), const(b_rt.shape),
            const(hmat.shape), const(hmask.shape), const(dmask.shape), const(ltri.shape),
        ],
        out_specs=[
            pl.BlockSpec((1, ts, d), lambda b, s: (b, s, 0)),
            pl.BlockSpec((1, ts * ROW_TILES, LANES), lambda b, s: (b, s, 0)),
            pl.BlockSpec((ts, LANES), lambda b, s: (b * n_s + s, 0)),
            pl.BlockSpec((SUBLANES, LANES), lambda b, s: (0, 0)),
        ],
        out_shape=[
            jax.ShapeDtypeStruct((bsz, seq, d), F32),
            jax.ShapeDtypeStruct((bsz, seq * ROW_TILES, LANES), jnp.uint32),
            jax.ShapeDtypeStruct((bsz * seq, LANES), F32),
            jax.ShapeDtypeStruct((SUBLANES, LANES), F32),
        ],
        scratch_shapes=[
            pltpu.VMEM((ts, w_in.shape[1]), F32),
            pltpu.VMEM((CHUNK, dh), F32),
            pltpu.VMEM((CHUNK, dh), F32),
            pltpu.VMEM((CHUNK, dh), F32),
            pltpu.VMEM((CHUNK, dh), F32),
            pltpu.VMEM(((1 + hmask.shape[0]) * CHUNK, dh), F32),
            pltpu.VMEM((CHUNK, dh), F32),
            pltpu.VMEM((ts, dh + dr), BF16),
            pltpu.VMEM((nh, HEAD_DIM, HEAD_DIM), F32),
            pltpu.VMEM((SUBLANES, dr), F32),
            pltpu.VMEM((ts + SUBLANES, dr), F32),
            pltpu.VMEM((SUBLANES, LANES), F32),
        ],
        compiler_params=pltpu.CompilerParams(
            dimension_semantics=("arbitrary", "arbitrary"), vmem_limit_bytes=VMEM_LIMIT_BYTES),
        name="mix_route",
    )(x, mod, p1024, p512, w_in, w_ri, w_out, w_rt, b_rt, hmat, hmask, dmask, ltri)


def _dispatch_kernel(tail_ref, dest_ref, h2_ref, xb_ref, zero_s, sem, zsem):
    tt = h2_ref.shape[0]

    @pl.when(pl.program_id(0) == 0)
    def _():
        zero_s[...] = jnp.zeros_like(zero_s)

        def tail_copy(e):
            row0 = pl.multiple_of(jnp.maximum(tail_ref[e], 0), MOE_BLOCK)
            return pltpu.make_async_copy(zero_s, xb_ref.at[pl.ds(row0, MOE_BLOCK)], zsem)

        for e in range(tail_ref.shape[0]):
            @pl.when(tail_ref[e] >= 0)
            def _():
                tail_copy(e).start()
        for e in range(tail_ref.shape[0]):
            @pl.when(tail_ref[e] >= 0)
            def _():
                tail_copy(e).wait()

    def row_copy(t, slot):
        return pltpu.make_async_copy(h2_ref.at[pl.ds(t, 1)], xb_ref.at[pl.ds(slot, 1)], sem)

    def issue(t, carry):
        row_copy(t, dest_ref[0, 0, 2 * t]).start()
        row_copy(t, dest_ref[0, 0, 2 * t + 1]).start(priority=1)
        return carry

    lax.fori_loop(0, tt, issue, 0, unroll=8)

    def drain(t, carry):
        row_copy(0, 0).wait()
        row_copy(0, 0).wait()
        return carry

    lax.fori_loop(0, tt, drain, 0, unroll=8)


def _dispatch(tails, dest, h2, n_slots):
    n_tok = h2.shape[0]
    row = h2.shape[1:]
    tt = dest.shape[2] // TOP_K
    grid_spec = pltpu.PrefetchScalarGridSpec(
        num_scalar_prefetch=1,
        grid=(n_tok // tt,),
        in_specs=[
            pl.BlockSpec((1, 1, TOP_K * tt), lambda i, tl: (i, 0, 0), memory_space=pltpu.SMEM),
            pl.BlockSpec((tt,) + row, lambda i, tl: (i, 0, 0)),
        ],
        out_specs=pl.BlockSpec(memory_space=pl.ANY),
        scratch_shapes=[pltpu.VMEM((MOE_BLOCK,) + row, h2.dtype),
                        pltpu.SemaphoreType.DMA(()), pltpu.SemaphoreType.DMA(())],
    )
    return pl.pallas_call(
        _dispatch_kernel,
        grid_spec=grid_spec,
        out_shape=jax.ShapeDtypeStruct((n_slots,) + row, h2.dtype),
        compiler_params=pltpu.CompilerParams(dimension_semantics=("arbitrary",)),
        name="dispatch_rows",
    )(tails, dest, h2)


def _ffn_kernel(first_ref, nblk_ref, spare_ref, w1_ref, w3_ref, w2_ref, xb_ref, yb_ref,
                w1b, w3b, w2b, xbuf, ybuf, xt2, yt2, xt1, yt1, xsem, ysem, tsem, zsem):
    e = pl.program_id(0)
    blk_rows = xt1.shape[0]
    grp_rows = xbuf.shape[1]
    nb = nblk_ref[e]
    n_grp = nb // FFN_GROUP
    n_tail = nb - n_grp * FFN_GROUP
    row0 = first_ref[e] * blk_rows
    tail0 = row0 + n_grp * grp_rows

    def rows_at(start, n):
        return pl.ds(pl.multiple_of(start, blk_rows), n)

    def x_copy(i, slot):
        return pltpu.make_async_copy(xb_ref.at[rows_at(row0 + i * grp_rows, grp_rows)], xbuf.at[slot], xsem.at[slot])

    def y_copy(i, slot):
        return pltpu.make_async_copy(ybuf.at[slot], yb_ref.at[rows_at(row0 + i * grp_rows, grp_rows)], ysem.at[slot])

    def tail_piece(p, off_blocks):
        x_t, y_t = tails[p]
        hbm_rows = rows_at(tail0 + off_blocks * blk_rows, x_t.shape[0])
        return (pltpu.make_async_copy(xb_ref.at[hbm_rows], x_t, tsem.at[p, 0]),
                pltpu.make_async_copy(y_t, yb_ref.at[hbm_rows], tsem.at[p, 1]))

    def expert_rows(x_view, y_view):
        xv = _unpack_rows(_load_row_tiles(x_view)).astype(BF16)
        a = _dot(xv, w1b[...])
        g = _dot(xv, w3b[...])
        hact = (a * _sigmoid(a) * g).astype(BF16)
        _store_row_tiles(y_view, _pack_rows(_dot(hact, w2b[...])))

    @pl.when(e == 0)
    def _():
        yt1[...] = jnp.zeros_like(yt1)

        def spare_copy(j):
            r = pl.multiple_of(jnp.maximum(spare_ref[j], 0) * blk_rows, blk_rows)
            return pltpu.make_async_copy(yt1, yb_ref.at[pl.ds(r, blk_rows)], zsem)

        for j in range(spare_ref.shape[0]):
            @pl.when(spare_ref[j] >= 0)
            def _():
                spare_copy(j).start()
        for j in range(spare_ref.shape[0]):
            @pl.when(spare_ref[j] >= 0)
            def _():
                spare_copy(j).wait()

    tails = ((xt2, yt2), (xt1, yt1))
    pieces = ((0, 0, n_tail >= 2), (1, n_tail - n_tail % 2, n_tail % 2 == 1))

    @pl.when(nb > 0)
    def _():
        @pl.when(n_grp > 0)
        def _():
            x_copy(0, 0).start()

        for p, off, present in pieces:
            @pl.when(present)
            def _():
                tail_piece(p, off)[0].start()

        w1b[...] = w1_ref[0].astype(BF16)
        w3b[...] = w3_ref[0].astype(BF16)
        w2b[...] = w2_ref[0].astype(BF16)

        def body(i, carry):
            slot = i % 2
            x_copy(i, slot).wait()

            @pl.when(i + 1 < n_grp)
            def _():
                x_copy(i + 1, 1 - slot).start()

            @pl.when(i >= 2)
            def _():
                y_copy(i - 2, slot).wait()

            expert_rows(xbuf.at[slot], ybuf.at[slot])
            y_copy(i, slot).start()
            return carry

        lax.fori_loop(0, n_grp, body, 0)

        for p, off, present in pieces:
            @pl.when(present)
            def _():
                x_cp, y_cp = tail_piece(p, off)
                x_cp.wait()
                expert_rows(*tails[p])
                y_cp.start()

        @pl.when(n_grp >= 2)
        def _():
            y_copy(n_grp - 2, n_grp % 2).wait()

        @pl.when(n_grp >= 1)
        def _():
            y_copy(n_grp - 1, (n_grp - 1) % 2).wait()

        for p, off, present in pieces:
            @pl.when(present)
            def _():
                tail_piece(p, off)[1].wait()


def _ffn(first_blk, n_blk, spare_blk, xb, w1, w3, w2):
    n_rows, lanes = xb.shape
    blk_rows = MOE_BLOCK * ROW_TILES
    n_exp, d, de = w1.shape
    assert FFN_GROUP == 4

    def rows_buf(*lead):
        return pltpu.VMEM(lead + (lanes,), xb.dtype)

    def w_map(e, *_):
        return (e, 0, 0)

    grid_spec = pltpu.PrefetchScalarGridSpec(
        num_scalar_prefetch=3,
        grid=(n_exp,),
        in_specs=[
            pl.BlockSpec((1, d, de), w_map),
            pl.BlockSpec((1, d, de), w_map),
            pl.BlockSpec((1, de, d), w_map),
            pl.BlockSpec(memory_space=pl.ANY),
        ],
        out_specs=pl.BlockSpec(memory_space=pl.ANY),
        scratch_shapes=[pltpu.VMEM((d, de), BF16), pltpu.VMEM((d, de), BF16), pltpu.VMEM((de, d), BF16),
                        rows_buf(2, FFN_GROUP * blk_rows), rows_buf(2, FFN_GROUP * blk_rows),
                        rows_buf(2 * blk_rows), rows_buf(2 * blk_rows), rows_buf(blk_rows), rows_buf(blk_rows),
                        pltpu.SemaphoreType.DMA((2,)), pltpu.SemaphoreType.DMA((2,)),
                        pltpu.SemaphoreType.DMA((2, 2)), pltpu.SemaphoreType.DMA(())],
    )
    return pl.pallas_call(
        _ffn_kernel,
        grid_spec=grid_spec,
        out_shape=jax.ShapeDtypeStruct((n_rows, lanes), xb.dtype),
        compiler_params=pltpu.CompilerParams(
            dimension_semantics=("arbitrary",), vmem_limit_bytes=VMEM_LIMIT_BYTES),
        name="expert_ffn",
    )(first_blk, n_blk, spare_blk, w1, w3, w2, xb)


def _combine_kernel(dest_ref, dest_next_ref, yb_ref, route_ref, x1_ref, mod_ref, g_ref, out_ref, ybuf, sem):
    tt = x1_ref.shape[0]
    i = pl.program_id(0)
    cur = i % 2

    def row_copy(slot, buf, k, t):
        return pltpu.make_async_copy(yb_ref.at[slot], ybuf.at[buf, k, pl.ds(t * ROW_TILES, ROW_TILES)], sem.at[buf])

    def issue_rows(d_ref, buf, t0, n):
        for t in range(n):
            row_copy(d_ref[0, 0, 2 * (t0 + t)], buf, 0, t0 + t).start()
            row_copy(d_ref[0, 0, 2 * (t0 + t) + 1], buf, 1, t0 + t).start(priority=1)

    @pl.when(i == 0)
    def _():
        def body(c, carry):
            issue_rows(dest_ref, 0, c * COMBINE_ROWS, COMBINE_ROWS)
            return carry
        lax.fori_loop(0, tt // COMBINE_ROWS, body, 0)

    def drain(t, carry):
        row_copy(0, cur, 0, 0).wait()
        row_copy(0, cur, 0, 0).wait()
        return carry

    lax.fori_loop(0, tt, drain, 0, unroll=8)

    gt2 = mod_ref[0, 5:6, :]
    g_post = g_ref[...]

    def finish_rows(t0):
        rows = pl.ds(pl.multiple_of(t0, COMBINE_ROWS), COMBINE_ROWS)
        words = pl.ds(pl.multiple_of(t0 * ROW_TILES, COMBINE_ROWS * ROW_TILES), COMBINE_ROWS * ROW_TILES)
        w0 = route_ref[rows, _R_W0:_R_W0 + 1]
        w1 = route_ref[rows, _R_W1:_R_W1 + 1]
        y = (_unpack_rows(_load_row_tiles(ybuf.at[cur, 0, words])) * w0
             + _unpack_rows(_load_row_tiles(ybuf.at[cur, 1, words])) * w1)
        out_ref[rows, :] = x1_ref[rows, :] + gt2 * _rms(y, g_post)

    @pl.when(i + 1 < pl.num_programs(0))
    def _():
        def body(c, carry):
            issue_rows(dest_next_ref, 1 - cur, c * COMBINE_ROWS, COMBINE_ROWS)
            finish_rows(c * COMBINE_ROWS)
            return carry
        lax.fori_loop(0, tt // COMBINE_ROWS, body, 0)

    @pl.when(i + 1 == pl.num_programs(0))
    def _():
        def body(c, carry):
            finish_rows(c * COMBINE_ROWS)
            return carry
        lax.fori_loop(0, tt // COMBINE_ROWS, body, 0)


def _combine(dest, yb, route, x1, mod, g_post_ffn, seq):
    n_tok, d = x1.shape
    tt = dest.shape[2] // TOP_K
    n_steps = n_tok // tt
    per_b = seq // tt
    return pl.pallas_call(
        _combine_kernel,
        grid=(n_steps,),
        in_specs=[
            pl.BlockSpec((1, 1, TOP_K * tt), lambda i: (i, 0, 0), memory_space=pltpu.SMEM),
            pl.BlockSpec((1, 1, TOP_K * tt), lambda i: (jnp.minimum(i + 1, n_steps - 1), 0, 0),
                         memory_space=pltpu.SMEM),
            pl.BlockSpec(memory_space=pl.ANY),
            pl.BlockSpec((tt, LANES), lambda i: (i, 0)),
            pl.BlockSpec((tt, d), lambda i: (i, 0)),
            pl.BlockSpec((1, mod.shape[1], d), lambda i: (i // per_b, 0, 0)),
            pl.BlockSpec((1, d), lambda i: (0, 0)),
        ],
        out_specs=pl.BlockSpec((tt, d), lambda i: (i, 0)),
        out_shape=jax.ShapeDtypeStruct((n_tok, d), F32),
        scratch_shapes=[pltpu.VMEM((2, TOP_K, tt * ROW_TILES, LANES), yb.dtype), pltpu.SemaphoreType.DMA((2,))],
        compiler_params=pltpu.CompilerParams(
            dimension_semantics=("arbitrary",), vmem_limit_bytes=VMEM_LIMIT_BYTES),
        name="combine_rows",
    )(dest, dest, yb, route, x1, mod, g_post_ffn)


def _block_diag(w):
    nb, bi, bo = w.shape
    eye = jnp.eye(nb, dtype=w.dtype)
    return (w[:, :, None, :] * eye[:, None, :, None]).reshape(nb * bi, nb * bo)


def kernel(x, c, w_ada, b_ada, g_pre_mix, g_post_mix, g_pre_ffn, g_post_ffn, w_in, hgrn_gamma, hgrn_norm_g, conv_w, conv_b, rg_w_r, rg_b_r, rg_w_i, rg_b_i, rg_lambda, rg_norm_g, w_out, w_router_group, b_router_group, w_router_expert, b_router_expert, w1, w3, w2):
    bsz, seq, d = x.shape
    depth = w_ada.shape[0]
    assert depth == 1 and hgrn_gamma.shape[0] == 2
    dh = hgrn_norm_g.shape[1]
    dr = rg_norm_g.shape[1]
    assert dh == dr and dh % (2 * HEAD_DIM) == 0 and seq % TILE_S == 0 and TILE_S % CHUNK == 0
    assert (dr // 2) % (dr // rg_w_r.shape[1]) == 0
    assert w_router_expert.shape[2] == N_EXPERTS and w_router_group.shape[2] == N_GROUPS
    n_tok = bsz * seq

    c8 = jnp.pad(c, ((0, SUBLANES - bsz), (0, 0)))
    p512 = jnp.concatenate([
        hgrn_gamma, hgrn_norm_g, conv_w[0], conv_b, rg_b_r, rg_b_i, rg_lambda, rg_norm_g,
        jnp.zeros((4, dh), F32)], axis=0)
    p1024 = jnp.concatenate([g_pre_mix, g_post_mix, g_pre_ffn, jnp.zeros((5, d), F32)], axis=0)
    w_ri = jnp.concatenate([_block_diag(rg_w_r[0]), _block_diag(rg_w_i[0])], axis=1).astype(BF16)
    w_rt = jnp.concatenate([w_router_expert[0], w_router_group[0],
                            jnp.zeros((d, LANES - N_EXPERTS - N_GROUPS), F32)], axis=1)
    b_rt = jnp.concatenate([b_router_expert[0], b_router_group[0],
                            jnp.zeros((LANES - N_EXPERTS - N_GROUPS,), F32)])[None, :]

    mod = _ada(c8, w_ada[0], b_ada[0][None, :])[:bsz].reshape(bsz, 6, d)

    x1, h2, route, cnt = _mix(x, mod, p1024, p512, w_in[0].astype(BF16), w_ri, w_out[0].astype(BF16), w_rt, b_rt)

    counts = cnt[0, :N_EXPERTS].astype(jnp.int32)
    pcounts = (counts + MOE_BLOCK - 1) // MOE_BLOCK * MOE_BLOCK
    pends = jnp.cumsum(pcounts)
    pstarts = pends - pcounts
    n_blocks = -(-(n_tok * TOP_K) // MOE_BLOCK) + N_EXPERTS
    eids = route[:, _R_E0:_R_E1 + 1].astype(jnp.int32)
    ranks = route[:, _R_RANK0:_R_RANK1 + 1].astype(jnp.int32)
    is_e = eids[:, :, None] == jnp.arange(N_EXPERTS, dtype=jnp.int32)
    dest = jnp.sum(jnp.where(is_e, pstarts, 0), axis=2) + ranks
    spare = pends[-1] + jnp.arange(N_EXPERTS, dtype=jnp.int32) * MOE_BLOCK
    tails = jnp.concatenate([jnp.where(pcounts > 0, pends - MOE_BLOCK, -1),
                             jnp.where(spare < n_blocks * MOE_BLOCK, spare, -1)]).astype(jnp.int32)

    n_slots = n_blocks * MOE_BLOCK
    dest_tiles = dest.reshape(n_tok // TILE_T, 1, TOP_K * TILE_T)
    xb = _dispatch(tails, dest_tiles, h2.reshape(n_tok, ROW_TILES, LANES), n_slots)
    spare_blk = jnp.where(spare < n_slots, spare // MOE_BLOCK, -1).astype(jnp.int32)
    yb = _ffn((pstarts // MOE_BLOCK).astype(jnp.int32), (pcounts // MOE_BLOCK).astype(jnp.int32), spare_blk,
              xb.reshape(n_slots * ROW_TILES, LANES), w1[0], w3[0], w2[0])
    out = _combine(dest_tiles, yb.reshape(n_slots, ROW_TILES, LANES), route,
                   x1.reshape(n_tok, d), mod, g_post_ffn, seq)
    return out.reshape(bsz, seq, d)
```
